```python
import math
import jax, jax.numpy as jnp
from jax import lax
import numpy as np

D_MODEL = 1024
BATCH = 8
SEQ = 2048
DEPTH = 2
DEC_BATCH = 128
DEC_SEQ = 8
PAST_LEN = 2048
PAGE_SIZE = 128

N_A = (DEPTH + 1) // 2
N_C = DEPTH // 2
D_A = D_MODEL // 2
H_A = 4
DH_A = D_A // H_A
MLSTM_CHUNK = 64
D_B = D_MODEL // 2
CONV_GROUPS = 8
CONV_W = 3
A_SPLITS = (D_A, 2 * D_A, 3 * D_A, 4 * D_A, 4 * D_A + H_A, 4 * D_A + 2 * H_A,
            4 * D_A + 2 * H_A + D_B, 4 * D_A + 2 * H_A + 2 * D_B)
A_COLS = 4 * D_A + 2 * H_A + 3 * D_B
H_C = 8
DH_C = D_MODEL // (2 * H_C)
D_C = 2 * H_C * DH_C
Q_BLOCK = 128
ROPE_THETA = 10000.0
N_MEM = 256
H_M = 4
DH_M = D_MODEL // H_M
N_GROUPS = 4
E_PER_GROUP = 8
TOP_K_IN_GROUP = 2
D_FF_EXPERT = D_MODEL // 4
EPS = 1e-6

kernel_name = "hybrid_mlstm_shortconv_diffattn_hmoe_step"


def rms_norm(x, g):
    xf = x.astype(jnp.float32)
    y = xf * lax.rsqrt(jnp.mean(xf * xf, axis=-1, keepdims=True) + EPS)
    return (y * g.astype(jnp.float32)).astype(x.dtype)


def rope(x, pos):
    half = x.shape[-1] // 2
    inv_freq = jnp.exp(-math.log(ROPE_THETA) * jnp.arange(half, dtype=jnp.float32) / half)
    ang = pos.astype(jnp.float32)[:, None] * inv_freq[None, :]
    bshape = (1, pos.shape[0]) + (1,) * (x.ndim - 3) + (half,)
    cos = jnp.cos(ang).reshape(bshape)
    sin = jnp.sin(ang).reshape(bshape)
    xf = x.astype(jnp.float32)
    x1, x2 = xf[..., :half], xf[..., half:]
    return jnp.concatenate([x1 * cos - x2 * sin, x2 * cos + x1 * sin], axis=-1).astype(x.dtype)


def mlstm_chunk(state, q, k, v, ig, lf):
    c0, n0, m0 = state
    L = q.shape[2]
    b = jnp.cumsum(lf, axis=-1)
    causal = jnp.tril(jnp.ones((L, L), dtype=bool))
    log_d = jnp.where(causal, b[..., :, None] - b[..., None, :] + ig[..., None, :], -jnp.inf)
    log_inter = b + m0[..., None]
    m_t = jnp.maximum(log_inter, jnp.max(log_d, axis=-1))
    d = jnp.exp(log_d - m_t[..., None])
    inter = jnp.exp(log_inter - m_t)
    s = jnp.einsum('bhtd,bhsd->bhts', q, k) * d
    num = jnp.einsum('bhts,bhsd->bhtd', s, v) + inter[..., None] * jnp.einsum('bhvk,bhtk->bhtv', c0, q)
    den = jnp.sum(s, axis=-1) + inter * jnp.einsum('bhk,bhtk->bht', n0, q)
    h = num / jnp.maximum(jnp.abs(den), jnp.exp(-m_t))[..., None]
    m_new = m_t[..., -1]
    w_last = d[..., -1, :]
    decay = inter[..., -1]
    c_new = decay[..., None, None] * c0 + jnp.einsum('bhsv,bhsk->bhvk', v * w_last[..., None], k)
    n_new = decay[..., None] * n0 + jnp.einsum('bhs,bhsk->bhk', w_last, k)
    return (c_new, n_new, m_new), h


def mlstm_conv_mixer(xn, w_in, gate_b, head_g, conv_w, w_out, state, conv_buf):
    f32 = jnp.float32
    B, T, _ = xn.shape
    z = xn @ w_in
    q, k, v, o, ig, fg, gb, gc, hb = jnp.split(z, A_SPLITS, axis=-1)

    def heads(t):
        return t.reshape(B, T, H_A, DH_A).transpose(0, 2, 1, 3).astype(f32)

    qh, kh, vh = heads(q), heads(k) * (DH_A ** -0.5), heads(v)
    gate_b = gate_b.astype(f32)
    igh = (ig.astype(f32) + gate_b[0]).transpose(0, 2, 1)
    lfh = jax.nn.log_sigmoid(fg.astype(f32) + gate_b[1]).transpose(0, 2, 1)
    if state is None:
        nc = T // MLSTM_CHUNK

        def chunks(t):
            return jnp.moveaxis(t.reshape(t.shape[:2] + (nc, MLSTM_CHUNK) + t.shape[3:]), 2, 0)

        state0 = (jnp.zeros((B, H_A, DH_A, DH_A), f32), jnp.zeros((B, H_A, DH_A), f32), jnp.zeros((B, H_A), f32))
        new_state, hs = lax.scan(lambda st, inp: mlstm_chunk(st, *inp), state0,
                                 (chunks(qh), chunks(kh), chunks(vh), chunks(igh), chunks(lfh)))
        h = jnp.moveaxis(hs, 0, 2).reshape(B, H_A, T, DH_A)
    else:
        st = (state[0].astype(f32), state[1].astype(f32), state[2].astype(f32))
        new_state, h = mlstm_chunk(st, qh, kh, vh, igh, lfh)
    h = h.transpose(0, 2, 1, 3)
    h = h * lax.rsqrt(jnp.mean(h * h, axis=-1, keepdims=True) + EPS) * head_g.astype(f32).reshape(H_A, DH_A)
    y_a = (jax.nn.sigmoid(o.astype(f32)) * h.reshape(B, T, D_A)).astype(xn.dtype)
    u = gc * hb
    prefix = jnp.zeros((B, CONV_W - 1, D_B), u.dtype) if conv_buf is None else conv_buf.astype(u.dtype)
    up = jnp.concatenate([prefix, u], axis=1)
    conv = up[:, 0:T] * conv_w[0]
    for j in range(1, CONV_W):
        conv = conv + up[:, j:j + T] * conv_w[j]
    y_b = gb * conv
    new_buf = up[:, -(CONV_W - 1):]
    y = jnp.concatenate([y_a, y_b], axis=-1) @ w_out
    return y, new_state, new_buf


def diff_softmax_mix(q, k, v, mask, lam, scale):
    s = jnp.einsum('bqhmd,bkhmd->bhmqk', q, k).astype(jnp.float32) * scale
    s = jnp.where(mask, s, -jnp.inf)
    p = jax.nn.softmax(s, axis=-1)
    a = p[:, :, 0] - lam * p[:, :, 1]
    return jnp.einsum('bhqk,bkhe->bqhe', a.astype(v.dtype), v)


def diff_attention(xn, w_qkv, lam_vecs, subln_g, w_out, layer_idx, past):
    B, T, _ = xn.shape
    q, k, v = jnp.split(xn @ w_qkv, 3, axis=-1)
    q = q.reshape(B, T, H_C, 2, DH_C)
    k = k.reshape(B, T, H_C, 2, DH_C)
    v = v.reshape(B, T, H_C, 2 * DH_C)
    offset = 0 if past is None else PAST_LEN
    pos = offset + jnp.arange(T)
    q = rope(q, pos)
    k = rope(k, pos)
    lam_init = 0.8 - 0.6 * math.exp(-0.3 * layer_idx)
    lv = lam_vecs.astype(jnp.float32)
    lam = jnp.exp(jnp.sum(lv[0] * lv[1])) - jnp.exp(jnp.sum(lv[2] * lv[3])) + lam_init
    scale = DH_C ** -0.5
    if past is None:
        nb = T // Q_BLOCK
        qb = jnp.moveaxis(q.reshape(B, nb, Q_BLOCK, H_C, 2, DH_C), 1, 0)
        k_pos = jnp.arange(T)

        def one_block(args):
            qi, bi = args
            q_pos = bi * Q_BLOCK + jnp.arange(Q_BLOCK)
            return diff_softmax_mix(qi, k, v, k_pos[None, :] <= q_pos[:, None], lam, scale)

        o = lax.map(one_block, (qb, jnp.arange(nb)))
        o = jnp.moveaxis(o, 0, 1).reshape(B, T, H_C, 2 * DH_C)
    else:
        k_past, v_past = past
        k_all = jnp.concatenate([k_past.astype(k.dtype), k], axis=1)
        v_all = jnp.concatenate([v_past.astype(v.dtype), v], axis=1)
        k_pos = jnp.arange(k_all.shape[1])
        mask = k_pos[None, :] <= pos[:, None]
        o = diff_softmax_mix(q, k_all, v_all, mask, lam, scale)
    o = rms_norm(o, subln_g) * (1.0 - lam_init)
    y = o.reshape(B, T, D_C) @ w_out
    return y, k, v


def mem_attention(xn, mem_k, mem_v, w_q, w_o):
    B, T, _ = xn.shape
    q = (xn @ w_q).reshape(B, T, H_M, DH_M)
    s = jnp.einsum('bqhd,bmhd->bhqm', q, mem_k).astype(jnp.float32) * (DH_M ** -0.5)
    p = jax.nn.softmax(s, axis=-1)
    o = jnp.einsum('bhqm,bmhd->bqhd', p.astype(mem_v.dtype), mem_v)
    return o.reshape(B, T, D_MODEL) @ w_o


def hier_moe(xn, w_rg, b_rg, w_re, b_re, w_gate, w_up, w_down):
    f32 = jnp.float32
    B, T, D = xn.shape
    x = xn.reshape(B * T, D)
    g_logits = (x @ w_rg).astype(f32) + b_rg.astype(f32)
    g_prob = jax.nn.softmax(g_logits, axis=-1)
    g_sel = jnp.argmax(g_logits, axis=-1)
    g_w = jnp.take_along_axis(g_prob, g_sel[:, None], axis=-1)
    e_logits = ((x @ w_re).astype(f32) + b_re.astype(f32)).reshape(-1, N_GROUPS, E_PER_GROUP)
    e_logits = jnp.take_along_axis(e_logits, g_sel[:, None, None], axis=1)[:, 0]
    e_prob = jax.nn.softmax(e_logits, axis=-1)
    top_v, top_i = lax.top_k(e_prob, TOP_K_IN_GROUP)
    top_v = top_v / jnp.sum(top_v, axis=-1, keepdims=True)
    e_w = jnp.sum(jax.nn.one_hot(top_i, E_PER_GROUP, dtype=f32) * top_v[..., None], axis=1)
    gate = jax.nn.one_hot(g_sel, N_GROUPS, dtype=f32)[:, :, None] * (g_w * e_w)[:, None, :]
    y = jnp.zeros_like(x)
    for g in range(N_GROUPS):
        h = jax.nn.silu(jnp.einsum('nd,edf->nef', x, w_gate[g])) * jnp.einsum('nd,edf->nef', x, w_up[g])
        y = y + jnp.einsum('nef,efd->nd', h * gate[:, g, :, None].astype(h.dtype), w_down[g])
    return y.reshape(B, T, D)


def trunk(x, mem_k, mem_v, st_c, st_n, st_m, st_conv, cache_k, cache_v, page_table, p):
    is_prompt = st_c is None
    out_c, out_n, out_m, out_conv, out_k, out_v = [], [], [], [], [], []
    ia = 0
    ic = 0
    for layer in range(DEPTH):
        h = rms_norm(x, p['norm_mix_g'][layer])
        if layer % 2 == 0:
            state = None if is_prompt else (st_c[ia], st_n[ia], st_m[ia])
            buf = None if is_prompt else st_conv[ia]
            y, (c_new, n_new, m_new), buf_new = mlstm_conv_mixer(
                h, p['w_in_a'][ia], p['gate_b_a'][ia], p['head_norm_g_a'][ia], p['conv_w_b'][ia],
                p['w_out_a'][ia], state, buf)
            out_c.append(c_new.astype(x.dtype))
            out_n.append(n_new.astype(x.dtype))
            out_m.append(m_new.astype(x.dtype))
            out_conv.append(buf_new)
            ia += 1
        else:
            if is_prompt:
                past = None
            else:
                nb, npg = page_table.shape
                k_past = cache_k[ic, page_table].reshape(nb, npg * PAGE_SIZE, H_C, 2, DH_C)
                v_past = cache_v[ic, page_table].reshape(nb, npg * PAGE_SIZE, H_C, 2 * DH_C)
                past = (k_past, v_past)
            y, k_new, v_new = diff_attention(h, p['w_qkv_c'][ic], p['lambda_c'][ic], p['subln_g_c'][ic],
                                             p['w_out_c'][ic], layer, past)
            out_k.append(k_new)
            out_v.append(v_new)
            ic += 1
        x = x + y
        x = x + mem_attention(rms_norm(x, p['norm_mem_g'][layer]), mem_k[layer], mem_v[layer],
                              p['w_mem_q'][layer], p['w_mem_o'][layer])
        x = x + hier_moe(rms_norm(x, p['norm_ffn_g'][layer]), p['w_router_g'][layer], p['b_router_g'][layer],
                         p['w_router_e'][layer], p['b_router_e'][layer], p['w_exp_gate'][layer],
                         p['w_exp_up'][layer], p['w_exp_down'][layer])
    y = rms_norm(x, p['final_norm_g'])
    return y, (jnp.stack(out_c), jnp.stack(out_n), jnp.stack(out_m), jnp.stack(out_conv),
               jnp.stack(out_k), jnp.stack(out_v))


def setup_inputs(seed: int = 0) -> dict:
    key = jax.random.key(seed)
    keys = list(jax.random.split(key, 48))

    def nrm(shape, scale=1.0):
        return scale * jax.random.normal(keys.pop(), shape, jnp.float32)

    n_pages = PAST_LEN // PAGE_SIZE
    n_pool = (DEC_BATCH * n_pages * 5 + 3) // 4
    perm = jax.random.permutation(keys.pop(), n_pool)
    page_table = perm[:DEC_BATCH * n_pages].reshape(DEC_BATCH, n_pages).astype(jnp.int32)
    gate_b_a = jnp.stack([nrm((N_A, H_A), 0.1),
                          jnp.broadcast_to(jnp.linspace(3.0, 6.0, H_A), (N_A, H_A)) + nrm((N_A, H_A), 0.1)], axis=1)
    sd = D_MODEL ** -0.5
    return {
        "x_prompt": nrm((BATCH, SEQ, D_MODEL)),
        "x_sample": nrm((DEC_BATCH, DEC_SEQ, D_MODEL)),
        "state_mlstm_C": nrm((N_A, DEC_BATCH, H_A, DH_A, DH_A), 0.1),
        "state_mlstm_n": nrm((N_A, DEC_BATCH, H_A, DH_A)),
        "state_mlstm_m": nrm((N_A, DEC_BATCH, H_A)),
        "state_conv": nrm((N_A, DEC_BATCH, CONV_W - 1, D_B)),
        "cache_k": nrm((N_C, n_pool, PAGE_SIZE, H_C, 2, DH_C)),
        "cache_v": nrm((N_C, n_pool, PAGE_SIZE, H_C, 2 * DH_C)),
        "page_table": page_table,
        "cache_mem_k": nrm((DEPTH, DEC_BATCH, N_MEM, H_M, DH_M)),
        "cache_mem_v": nrm((DEPTH, DEC_BATCH, N_MEM, H_M, DH_M)),
        "mem_prompt": nrm((BATCH, N_MEM, D_MODEL)),
        "norm_mix_g": 1.0 + nrm((DEPTH, D_MODEL), 0.02),
        "norm_mem_g": 1.0 + nrm((DEPTH, D_MODEL), 0.02),
        "norm_ffn_g": 1.0 + nrm((DEPTH, D_MODEL), 0.02),
        "final_norm_g": 1.0 + nrm((D_MODEL,), 0.02),
        "w_in_a": nrm((N_A, D_MODEL, A_COLS), sd),
        "gate_b_a": gate_b_a,
        "head_norm_g_a": 1.0 + nrm((N_A, D_A), 0.02),
        "conv_w_b": nrm((N_A, CONV_W, D_B), CONV_W ** -0.5),
        "w_out_a": nrm((N_A, D_A + D_B, D_MODEL), (D_A + D_B) ** -0.5),
        "w_qkv_c": nrm((N_C, D_MODEL, 3 * D_C), sd),
        "lambda_c": nrm((N_C, 4, DH_C), 0.1),
        "subln_g_c": 1.0 + nrm((N_C, 2 * DH_C), 0.02),
        "w_out_c": nrm((N_C, D_C, D_MODEL), D_C ** -0.5),
        "w_mem_q": nrm((DEPTH, D_MODEL, D_MODEL), sd),
        "w_mem_k": nrm((DEPTH, D_MODEL, D_MODEL), sd),
        "w_mem_v": nrm((DEPTH, D_MODEL, D_MODEL), sd),
        "w_mem_o": nrm((DEPTH, D_MODEL, D_MODEL), sd),
        "w_router_g": nrm((DEPTH, D_MODEL, N_GROUPS), sd),
        "b_router_g": nrm((DEPTH, N_GROUPS), 0.01),
        "w_router_e": nrm((DEPTH, D_MODEL, N_GROUPS * E_PER_GROUP), sd),
        "b_router_e": nrm((DEPTH, N_GROUPS * E_PER_GROUP), 0.01),
        "w_exp_gate": nrm((DEPTH, N_GROUPS, E_PER_GROUP, D_MODEL, D_FF_EXPERT), sd),
        "w_exp_up": nrm((DEPTH, N_GROUPS, E_PER_GROUP, D_MODEL, D_FF_EXPERT), sd),
        "w_exp_down": nrm((DEPTH, N_GROUPS, E_PER_GROUP, D_FF_EXPERT, D_MODEL), D_FF_EXPERT ** -0.5),
    }


def reference(x_prompt, x_sample, state_mlstm_C, state_mlstm_n, state_mlstm_m, state_conv,
              cache_k, cache_v, page_table, cache_mem_k, cache_mem_v, mem_prompt,
              norm_mix_g, norm_mem_g, norm_ffn_g, final_norm_g,
              w_in_a, gate_b_a, head_norm_g_a, conv_w_b, w_out_a,
              w_qkv_c, lambda_c, subln_g_c, w_out_c,
              w_mem_q, w_mem_k, w_mem_v, w_mem_o,
              w_router_g, b_router_g, w_router_e, b_router_e,
              w_exp_gate, w_exp_up, w_exp_down):
    p = dict(norm_mix_g=norm_mix_g, norm_mem_g=norm_mem_g, norm_ffn_g=norm_ffn_g, final_norm_g=final_norm_g,
             w_in_a=w_in_a, gate_b_a=gate_b_a, head_norm_g_a=head_norm_g_a, conv_w_b=conv_w_b, w_out_a=w_out_a,
             w_qkv_c=w_qkv_c, lambda_c=lambda_c, subln_g_c=subln_g_c, w_out_c=w_out_c,
             w_mem_q=w_mem_q, w_mem_o=w_mem_o,
             w_router_g=w_router_g, b_router_g=b_router_g, w_router_e=w_router_e, b_router_e=b_router_e,
             w_exp_gate=w_exp_gate, w_exp_up=w_exp_up, w_exp_down=w_exp_down)
    bp = mem_prompt.shape[0]
    p_mem_k = jnp.einsum('bmd,lde->lbme', mem_prompt, w_mem_k).reshape(DEPTH, bp, N_MEM, H_M, DH_M)
    p_mem_v = jnp.einsum('bmd,lde->lbme', mem_prompt, w_mem_v).reshape(DEPTH, bp, N_MEM, H_M, DH_M)
    y_prompt, (p_c, p_n, p_m, p_conv, p_k, p_v) = trunk(
        x_prompt, p_mem_k, p_mem_v, None, None, None, None, None, None, None, p)
    y_sample, (s_c, s_n, s_m, s_conv, s_k, s_v) = trunk(
        x_sample, cache_mem_k, cache_mem_v, state_mlstm_C, state_mlstm_n, state_mlstm_m, state_conv,
        cache_k, cache_v, page_table, p)
    return (y_prompt, y_sample, p_c, p_n, p_m, p_conv, p_k, p_v, p_mem_k, p_mem_v,
            s_c, s_n, s_m, s_conv, s_k, s_v)
```

```python
import functools
import math

import jax
import jax.numpy as jnp
from jax import lax
from jax.experimental import pallas as pl
from jax.experimental.pallas import tpu as pltpu

F32 = jnp.float32
BF16 = jnp.bfloat16
I32 = jnp.int32

EPS = 1e-6
NEG = -1e30
ROPE_THETA = 10000.0
PAGE_SIZE = 128
CONV_W = 3

D_MODEL = 1024
H_A = 4
DH_A = 128
D_A = H_A * DH_A
D_B = 512
H_C = 8
DH_C = 64
D_C = 2 * H_C * DH_C
N_MEM = 256
H_M = 4
DH_M = D_MODEL // H_M
N_GROUPS = 4
E_PER_GROUP = 8
N_EXPERTS = N_GROUPS * E_PER_GROUP
D_FF = D_MODEL // 4

LANES = 128
SUBLANES = 8
VMEM_LIMIT = 56 * 1024 * 1024

ROW_TILE = 512
MLSTM_L = 256
ATT_TQ = 512
MEM_TQ = 512
MOE_TM = 256
MOE_TE = 256


def _cparams(*sem):
    return pltpu.CompilerParams(dimension_semantics=sem, vmem_limit_bytes=VMEM_LIMIT)


def _rms(x, g):
    return x * lax.rsqrt(jnp.mean(x * x, axis=-1, keepdims=True) + EPS) * g


def _dot(a, b):
    return jnp.dot(a, b, preferred_element_type=F32)


def _dot_nt(a, b):
    return lax.dot_general(a, b, (((1,), (1,)), ((), ())), preferred_element_type=F32)


def _dot_tn(a, b):
    return lax.dot_general(a, b, (((0,), (0,)), ((), ())), preferred_element_type=F32)


def _split3(x):
    hi = x.astype(BF16)
    r1 = x - hi.astype(F32)
    mid = r1.astype(BF16)
    lo = (r1 - mid.astype(F32)).astype(BF16)
    return hi, mid, lo


def _full(shape):
    n = len(shape)
    return pl.BlockSpec(shape, lambda *_: (0,) * n)


def _inproj_kernel(x_ref, g_ref, w_ref, wg_ref, qkvo_ref, conv_ref, gate_ref):
    xn = _rms(x_ref[...], g_ref[...]).astype(BF16)
    for c in range(4 * D_A // 512):
        qkvo_ref[:, c * 512:(c + 1) * 512] = _dot(xn, w_ref[:, c * 512:(c + 1) * 512]).astype(qkvo_ref.dtype)
    off = 4 * D_A
    for c in range(3 * D_B // 512):
        conv_ref[:, c * 512:(c + 1) * 512] = _dot(
            xn, w_ref[:, off + c * 512:off + (c + 1) * 512]).astype(conv_ref.dtype)
    gate_ref[...] = _dot(xn, wg_ref[...])


def _inproj(x, g, w_main, w_gate, out_dtype, tm):
    n = x.shape[0]
    return pl.pallas_call(
        _inproj_kernel,
        grid=(n // tm,),
        in_specs=[pl.BlockSpec((tm, D_MODEL), lambda i: (i, 0)), _full((1, D_MODEL)),
                  _full(w_main.shape), _full(w_gate.shape)],
        out_specs=[pl.BlockSpec((tm, 4 * D_A), lambda i: (i, 0)),
                   pl.BlockSpec((tm, 3 * D_B), lambda i: (i, 0)),
                   pl.BlockSpec((tm, LANES), lambda i: (i, 0))],
        out_shape=[jax.ShapeDtypeStruct((n, 4 * D_A), out_dtype),
                   jax.ShapeDtypeStruct((n, 3 * D_B), out_dtype),
                   jax.ShapeDtypeStruct((n, LANES), F32)],
        compiler_params=_cparams("parallel"),
        name="inproj",
    )(x, g, w_main, w_gate)


def _norm_mm_kernel(x_ref, g_ref, w_ref, o_ref, *, scale):
    xn = _rms(x_ref[...], g_ref[...]).astype(BF16)
    o_ref[...] = (_dot(xn, w_ref[...]) * scale).astype(o_ref.dtype)


def _norm_mm(x, g, w, out_dtype, tm, scale=1.0):
    n = x.shape[0]
    nout = w.shape[1]
    return pl.pallas_call(
        functools.partial(_norm_mm_kernel, scale=scale),
        grid=(n // tm,),
        in_specs=[pl.BlockSpec((tm, D_MODEL), lambda i: (i, 0)), _full((1, D_MODEL)), _full(w.shape)],
        out_specs=pl.BlockSpec((tm, nout), lambda i: (i, 0)),
        out_shape=jax.ShapeDtypeStruct((n, nout), out_dtype),
        compiler_params=_cparams("parallel"),
        name="norm_mm",
    )(x, g, w)


def _mm_kernel(a_ref, w_ref, o_ref):
    o_ref[...] = _dot(a_ref[...].astype(BF16), w_ref[...]).astype(o_ref.dtype)


def _mm(a, w, out_dtype, tm):
    n, k = a.shape
    nout = w.shape[1]
    return pl.pallas_call(
        _mm_kernel,
        grid=(n // tm,),
        in_specs=[pl.BlockSpec((tm, k), lambda i: (i, 0)), _full(w.shape)],
        out_specs=pl.BlockSpec((tm, nout), lambda i: (i, 0)),
        out_shape=jax.ShapeDtypeStruct((n, nout), out_dtype),
        compiler_params=_cparams("parallel"),
        name="mm",
    )(a, w)


def _mm_res_kernel(x_ref, a_ref, w_ref, o_ref):
    o_ref[...] = x_ref[...] + _dot(a_ref[...].astype(BF16), w_ref[...])


def _mm_res(x, a, w, tm):
    n, k = a.shape
    return pl.pallas_call(
        _mm_res_kernel,
        grid=(n // tm,),
        in_specs=[pl.BlockSpec((tm, D_MODEL), lambda i: (i, 0)),
                  pl.BlockSpec((tm, k), lambda i: (i, 0)), _full(w.shape)],
        out_specs=pl.BlockSpec((tm, D_MODEL), lambda i: (i, 0)),
        out_shape=jax.ShapeDtypeStruct((n, D_MODEL), F32),
        compiler_params=_cparams("parallel"),
        name="mm_res",
    )(x, a, w)


def _qkv_rope_kernel(x_ref, g_ref, w_ref, cos_ref, sin_ref, q_ref, k_ref, v_ref, *, q_scale):
    xn = _rms(x_ref[...], g_ref[...]).astype(BF16)
    reps = D_C // LANES
    cos = jnp.concatenate([cos_ref[...]] * reps, axis=1)
    sin = jnp.concatenate([sin_ref[...]] * reps, axis=1)
    tm = xn.shape[0]
    lane = lax.broadcasted_iota(I32, (tm, D_C), 1)
    first_half = (lane % DH_C) < (DH_C // 2)

    def rope(t):
        partner = jnp.where(first_half, pltpu.roll(t, D_C - DH_C // 2, axis=1), pltpu.roll(t, DH_C // 2, axis=1))
        return t * cos + partner * sin

    q = rope(_dot(xn, w_ref[:, 0:D_C]))
    q_ref[...] = (q * q_scale).astype(q_ref.dtype)
    k_ref[...] = rope(_dot(xn, w_ref[:, D_C:2 * D_C]))
    v_ref[...] = _dot(xn, w_ref[:, 2 * D_C:3 * D_C])


def _qkv_rope(x, g, w, cos, sin, tab_map, q_dtype, tm):
    n = x.shape[0]
    row = lambda i: (i, 0)
    return pl.pallas_call(
        functools.partial(_qkv_rope_kernel, q_scale=DH_C ** -0.5),
        grid=(n // tm,),
        in_specs=[pl.BlockSpec((tm, D_MODEL), row), _full((1, D_MODEL)), _full(w.shape),
                  pl.BlockSpec((tm, LANES), tab_map), pl.BlockSpec((tm, LANES), tab_map)],
        out_specs=[pl.BlockSpec((tm, D_C), row)] * 3,
        out_shape=[jax.ShapeDtypeStruct((n, D_C), q_dtype),
                   jax.ShapeDtypeStruct((n, D_C), F32),
                   jax.ShapeDtypeStruct((n, D_C), F32)],
        compiler_params=_cparams("parallel"),
        name="qkv_rope",
    )(x, g, w, cos, sin)


def _log_sigmoid(x):
    return jnp.minimum(x, 0.0) - jnp.log(1.0 + jnp.exp(-jnp.abs(x)))


def _mlstm_kernel(qkvo_ref, conv_ref, gate_ref, gbias_ref, hg_ref, cw_ref,
                  c0_ref, n0_ref, m0_ref, cb0_ref,
                  y_ref, c_out, n_out, m_out, cb_out,
                  c_sc, n_sc, m_sc, cb_sc, *, L, nchunks):
    ci = pl.program_id(1)

    @pl.when(ci == 0)
    def _load_state():
        c_sc[...] = c0_ref[...]
        n_sc[...] = n0_ref[...]
        m_sc[...] = m0_ref[...]
        cb_sc[...] = cb0_ref[...]

    mm_dtype = BF16 if L % 16 == 0 else F32

    gates = gate_ref[...] + gbias_ref[...]
    lf = _log_sigmoid(gates)
    row = lax.broadcasted_iota(I32, (L, L), 0)
    col = lax.broadcasted_iota(I32, (L, L), 1)
    tri = row >= col
    tri_b = tri.astype(mm_dtype)
    bc = sum(_dot(tri_b, t.astype(mm_dtype)) for t in _split3(lf))
    dmat = gates - pltpu.roll(bc, LANES - H_A, axis=1)
    sel = (lax.broadcasted_iota(I32, (SUBLANES, LANES), 0)
           == lax.broadcasted_iota(I32, (SUBLANES, LANES), 1)).astype(mm_dtype)
    dt = sum(_dot_nt(sel, t.astype(mm_dtype)) for t in _split3(dmat))

    for h in range(H_A):
        bcol = bc[:, H_A + h:H_A + h + 1]
        igcol = gates[:, h:h + 1]
        rowv = dt[h:h + 1, :]
        m0 = m_sc[h:h + 1, 0:1]
        logd = jnp.where(tri, bcol + rowv, NEG)
        log_inter = bcol + m0
        m_t = jnp.maximum(log_inter, jnp.max(logd, axis=-1, keepdims=True))
        d = jnp.exp(logd - m_t)
        inter = jnp.exp(log_inter - m_t)

        q = qkvo_ref[:, h * DH_A:(h + 1) * DH_A]
        k = qkvo_ref[:, D_A + h * DH_A:D_A + (h + 1) * DH_A]
        v = qkvo_ref[:, 2 * D_A + h * DH_A:2 * D_A + (h + 1) * DH_A]
        o = qkvo_ref[:, 3 * D_A + h * DH_A:3 * D_A + (h + 1) * DH_A]
        qm, km, vm = q.astype(mm_dtype), k.astype(mm_dtype), v.astype(mm_dtype)
        qf, kf, vf = q.astype(F32), k.astype(F32), v.astype(F32)

        s = _dot_nt(qm, km) * d
        c0 = c_sc[h]
        n0 = n_sc[h:h + 1, :]
        num = _dot(s.astype(mm_dtype), vm) + inter * _dot_nt(qm, c0.astype(mm_dtype))
        den = jnp.sum(s, axis=-1, keepdims=True) + inter * jnp.sum(qf * n0, axis=-1, keepdims=True)
        hh = num / jnp.maximum(jnp.abs(den), jnp.exp(-m_t))

        b_last = bcol[L - 1:L, :]
        m_last = m_t[L - 1:L, :]
        wcol = jnp.exp(b_last - bcol + igcol - m_last)
        decay = inter[L - 1:L, :]
        c_sc[h] = decay * c0 + _dot_tn((vf * wcol).astype(mm_dtype), km)
        n_sc[h:h + 1, :] = decay * n0 + jnp.sum(wcol * kf, axis=0, keepdims=True)
        m_sc[h:h + 1, :] = jnp.broadcast_to(m_last, (1, LANES))

        hn = hh * lax.rsqrt(jnp.mean(hh * hh, axis=-1, keepdims=True) + EPS) * hg_ref[:, h * DH_A:(h + 1) * DH_A]
        y_ref[:, h * DH_A:(h + 1) * DH_A] = (jax.nn.sigmoid(o.astype(F32)) * hn).astype(y_ref.dtype)

    gbv = conv_ref[:, 0:D_B].astype(F32)
    u = conv_ref[:, D_B:2 * D_B].astype(F32) * conv_ref[:, 2 * D_B:3 * D_B].astype(F32)
    carry = cb_sc[...]
    rowi = lax.broadcasted_iota(I32, (L, D_B), 0)
    prev1 = carry[SUBLANES - 1:SUBLANES, :]
    prev2 = carry[SUBLANES - 2:SUBLANES - 1, :]
    um1 = jnp.where(rowi == 0, prev1, pltpu.roll(u, 1, axis=0))
    um2 = jnp.where(rowi == 0, prev2, jnp.where(rowi == 1, prev1, pltpu.roll(u, 2, axis=0)))
    conv = um2 * cw_ref[0:1, :] + um1 * cw_ref[1:2, :] + u * cw_ref[2:3, :]
    y_ref[:, D_A:D_A + D_B] = (gbv * conv).astype(y_ref.dtype)
    cb_sc[...] = u[L - SUBLANES:L, :]

    @pl.when(ci == nchunks - 1)
    def _store_state():
        c_out[...] = c_sc[...]
        n_out[...] = n_sc[...]
        m_out[...] = m_sc[...]
        cb_out[...] = cb_sc[...]


def _mlstm(qkvo, conv, gates, gbias, head_g, conv_w, c0, n0, m0, cb0, L, out_dtype):
    nb = c0.shape[0]
    t = qkvo.shape[0] // nb
    nchunks = t // L
    rows = lambda b, c: (b * nchunks + c, 0)
    st4 = lambda b, c: (b, 0, 0, 0)
    st3 = lambda b, c: (b, 0, 0)
    return pl.pallas_call(
        functools.partial(_mlstm_kernel, L=L, nchunks=nchunks),
        grid=(nb, nchunks),
        in_specs=[pl.BlockSpec((L, 4 * D_A), rows), pl.BlockSpec((L, 3 * D_B), rows),
                  pl.BlockSpec((L, LANES), rows),
                  _full((1, LANES)), _full((1, D_A)), _full((SUBLANES, D_B)),
                  pl.BlockSpec((None, H_A, DH_A, DH_A), st4),
                  pl.BlockSpec((None, SUBLANES, DH_A), st3),
                  pl.BlockSpec((None, SUBLANES, LANES), st3),
                  pl.BlockSpec((None, SUBLANES, D_B), st3)],
        out_specs=[pl.BlockSpec((L, D_A + D_B), rows),
                   pl.BlockSpec((None, H_A, DH_A, DH_A), st4),
                   pl.BlockSpec((None, SUBLANES, DH_A), st3),
                   pl.BlockSpec((None, SUBLANES, LANES), st3),
                   pl.BlockSpec((None, SUBLANES, D_B), st3)],
        out_shape=[jax.ShapeDtypeStruct((nb * t, D_A + D_B), out_dtype),
                   jax.ShapeDtypeStruct((nb, H_A, DH_A, DH_A), F32),
                   jax.ShapeDtypeStruct((nb, SUBLANES, DH_A), F32),
                   jax.ShapeDtypeStruct((nb, SUBLANES, LANES), F32),
                   jax.ShapeDtypeStruct((nb, SUBLANES, D_B), F32)],
        scratch_shapes=[pltpu.VMEM((H_A, DH_A, DH_A), F32), pltpu.VMEM((SUBLANES, DH_A), F32),
                        pltpu.VMEM((SUBLANES, LANES), F32), pltpu.VMEM((SUBLANES, D_B), F32)],
        compiler_params=_cparams("parallel", "arbitrary"),
        name="mlstm_conv",
    )(qkvo, conv, gates, gbias, head_g, conv_w, c0, n0, m0, cb0)


def _mem_attn_kernel(q_ref, k_ref, v_ref, o_ref):
    mm_dtype = BF16 if q_ref.shape[0] % 16 == 0 else F32
    for h in range(H_M):
        sl = slice(h * DH_M, (h + 1) * DH_M)
        q = q_ref[:, sl].astype(mm_dtype)
        k = k_ref[:, sl].astype(mm_dtype)
        v = v_ref[:, sl].astype(mm_dtype)
        s = _dot_nt(q, k)
        p = jnp.exp(s - jnp.max(s, axis=-1, keepdims=True))
        p = p / jnp.sum(p, axis=-1, keepdims=True)
        o_ref[:, sl] = _dot(p.astype(mm_dtype), v).astype(o_ref.dtype)


def _mem_attn(q, mem_k, mem_v, tq, out_dtype):
    nb = mem_k.shape[0]
    nq = q.shape[0] // nb // tq
    rows = lambda b, i: (b * nq + i, 0)
    kv = lambda b, i: (b, 0, 0)
    return pl.pallas_call(
        _mem_attn_kernel,
        grid=(nb, nq),
        in_specs=[pl.BlockSpec((tq, D_MODEL), rows),
                  pl.BlockSpec((None, N_MEM, D_MODEL), kv), pl.BlockSpec((None, N_MEM, D_MODEL), kv)],
        out_specs=pl.BlockSpec((tq, D_MODEL), rows),
        out_shape=jax.ShapeDtypeStruct((q.shape[0], D_MODEL), out_dtype),
        compiler_params=_cparams("parallel", "arbitrary"),
        name="mem_attn",
    )(q, mem_k, mem_v)


def _flash_diff_kernel(lam_ref, q_ref, k_ref, v_ref, g_ref, o_ref, m_sc, l_sc, acc_sc, *, tq, out_scale):
    qi = pl.program_id(2)
    ki = pl.program_id(3)

    @pl.when(ki == 0)
    def _init():
        m_sc[...] = jnp.full(m_sc.shape, NEG, F32)
        l_sc[...] = jnp.zeros(l_sc.shape, F32)
        acc_sc[...] = jnp.zeros(acc_sc.shape, F32)

    @pl.when(ki <= qi)
    def _step():
        q = q_ref[...]
        kb = k_ref[...].astype(BF16)
        vb = v_ref[...].astype(BF16)
        lane = lax.broadcasted_iota(I32, q.shape, 1)
        row = lax.broadcasted_iota(I32, (tq, tq), 0)
        col = lax.broadcasted_iota(I32, (tq, tq), 1)
        visible = (ki < qi) | (row >= col)
        for mi in range(2):
            qm = jnp.where((lane >= mi * DH_C) & (lane < (mi + 1) * DH_C), q, jnp.zeros_like(q))
            s = jnp.where(visible, _dot_nt(qm, kb), NEG)
            m_old = m_sc[mi]
            m_new = jnp.maximum(m_old, jnp.max(s, axis=-1, keepdims=True))
            p = jnp.exp(s - m_new)
            alpha = jnp.exp(m_old - m_new)
            l_sc[mi] = alpha * l_sc[mi] + jnp.sum(p, axis=-1, keepdims=True)
            acc_sc[mi] = alpha * acc_sc[mi] + _dot(p.astype(BF16), vb)
            m_sc[mi] = m_new

    @pl.when(ki == qi)
    def _finish():
        lam = lam_ref[0]
        o = acc_sc[0] / l_sc[0] - lam * (acc_sc[1] / l_sc[1])
        o_ref[...] = (_rms(o, g_ref[...]) * out_scale).astype(o_ref.dtype)


def _flash_diff(lam, q, k, v, subln_g, nb, out_scale, out_dtype):
    t = q.shape[0] // nb
    tq = min(ATT_TQ, t)
    nq = t // tq
    qmap = lambda b, h, i, j, lam_ref: (b * nq + i, h)
    kmap = lambda b, h, i, j, lam_ref: (b * nq + jnp.minimum(i, j), h)
    blk = 2 * DH_C
    grid_spec = pltpu.PrefetchScalarGridSpec(
        num_scalar_prefetch=1,
        grid=(nb, H_C, nq, nq),
        in_specs=[pl.BlockSpec((tq, blk), qmap), pl.BlockSpec((tq, blk), kmap), pl.BlockSpec((tq, blk), kmap),
                  pl.BlockSpec((1, blk), lambda b, h, i, j, lam_ref: (0, 0))],
        out_specs=pl.BlockSpec((tq, blk), qmap),
        scratch_shapes=[pltpu.VMEM((2, tq, 1), F32), pltpu.VMEM((2, tq, 1), F32), pltpu.VMEM((2, tq, blk), F32)],
    )
    return pl.pallas_call(
        functools.partial(_flash_diff_kernel, tq=tq, out_scale=out_scale),
        grid_spec=grid_spec,
        out_shape=jax.ShapeDtypeStruct((q.shape[0], D_C), out_dtype),
        compiler_params=_cparams("parallel", "parallel", "parallel", "arbitrary"),
        name="flash_diff",
    )(lam, q, k, v, subln_g)


def _paged_diff_kernel(pt_ref, lam_ref, q_ref, kc_ref, vc_ref, kn_ref, vn_ref, g_ref, o_ref,
                       qx_sc, m_sc, l_sc, acc_sc, *, n_pages, tdec, out_scale):
    p = pl.program_id(1)
    nrow = 2 * H_C * tdec

    @pl.when(p == 0)
    def _init():
        q = q_ref[...].astype(F32)
        qt = jnp.concatenate([q] * (2 * H_C), axis=0)
        r = lax.broadcasted_iota(I32, (nrow, D_C), 0)
        c = lax.broadcasted_iota(I32, (nrow, D_C), 1)
        qx_sc[...] = jnp.where((r // tdec) == (c // DH_C), qt, 0.0).astype(BF16)
        m_sc[...] = jnp.full(m_sc.shape, NEG, F32)
        l_sc[...] = jnp.zeros(l_sc.shape, F32)
        acc_sc[...] = jnp.zeros(acc_sc.shape, F32)

    def update(s, vb):
        m_old = m_sc[...]
        m_new = jnp.maximum(m_old, jnp.max(s, axis=-1, keepdims=True))
        pr = jnp.exp(s - m_new)
        alpha = jnp.exp(m_old - m_new)
        l_sc[...] = alpha * l_sc[...] + jnp.sum(pr, axis=-1, keepdims=True)
        acc_sc[...] = alpha * acc_sc[...] + _dot(pr.astype(vb.dtype), vb)
        m_sc[...] = m_new

    @pl.when(p < n_pages)
    def _page():
        update(_dot_nt(qx_sc[...], kc_ref[...].astype(BF16)), vc_ref[...].astype(BF16))

    @pl.when(p == n_pages)
    def _new_rows():
        kn = kn_ref[...]
        vn = vn_ref[...]
        s = _dot_nt(qx_sc[...].astype(F32), kn)
        r = lax.broadcasted_iota(I32, s.shape, 0)
        c = lax.broadcasted_iota(I32, s.shape, 1)
        s = jnp.where(c <= (r % tdec), s, NEG)
        update(s, vn)
        lam = lam_ref[0]
        of = acc_sc[...] / l_sc[...]
        for h in range(H_C):
            cs = slice(h * 2 * DH_C, (h + 1) * 2 * DH_C)
            o1 = of[(2 * h) * tdec:(2 * h + 1) * tdec, cs]
            o2 = of[(2 * h + 1) * tdec:(2 * h + 2) * tdec, cs]
            o = o1 - lam * o2
            o_ref[:, cs] = (_rms(o, g_ref[...]) * out_scale).astype(o_ref.dtype)


def _paged_diff(page_table, lam, q, cache_k, cache_v, k_new, v_new, subln_g, ic, tdec, out_scale):
    nb, n_pages = page_table.shape
    nrow = 2 * H_C * tdec
    rows = lambda b, p, pt, lm: (b, 0)
    page = lambda b, p, pt, lm: (ic, pt[b * n_pages + jnp.minimum(p, n_pages - 1)], 0, 0)
    grid_spec = pltpu.PrefetchScalarGridSpec(
        num_scalar_prefetch=2,
        grid=(nb, n_pages + 1),
        in_specs=[pl.BlockSpec((tdec, D_C), rows),
                  pl.BlockSpec((None, None, PAGE_SIZE, D_C), page),
                  pl.BlockSpec((None, None, PAGE_SIZE, D_C), page),
                  pl.BlockSpec((tdec, D_C), rows), pl.BlockSpec((tdec, D_C), rows),
                  pl.BlockSpec((1, 2 * DH_C), lambda b, p, pt, lm: (0, 0))],
        out_specs=pl.BlockSpec((tdec, D_C), rows),
        scratch_shapes=[pltpu.VMEM((nrow, D_C), BF16), pltpu.VMEM((nrow, 1), F32),
                        pltpu.VMEM((nrow, 1), F32), pltpu.VMEM((nrow, D_C), F32)],
    )
    return pl.pallas_call(
        functools.partial(_paged_diff_kernel, n_pages=n_pages, tdec=tdec, out_scale=out_scale),
        grid_spec=grid_spec,
        out_shape=jax.ShapeDtypeStruct((nb * tdec, D_C), F32),
        compiler_params=_cparams("parallel", "arbitrary"),
        name="paged_diff",
    )(page_table.reshape(-1), lam, q, cache_k, cache_v, k_new, v_new, subln_g)


def _router_kernel(x_ref, g_ref, wr_ref, br_ref, base_ref, xn_ref, ri_ref, rw_ref, cnt_ref, cnt_sc):
    i = pl.program_id(0)
    tm = x_ref.shape[0]

    @pl.when(i == 0)
    def _init():
        cnt_sc[...] = base_ref[...]

    xn = _rms(x_ref[...], g_ref[...])
    xn_ref[...] = xn
    lg = lax.dot_general(wr_ref[...], xn, (((1,), (1,)), ((), ())), preferred_element_type=F32,
                         precision=lax.Precision.HIGHEST) + br_ref[:, 0:1]
    sub = lax.broadcasted_iota(I32, (SUBLANES, tm), 0)
    gl = lg[N_EXPERTS:N_EXPERTS + SUBLANES, :]
    gmax = jnp.max(gl, axis=0, keepdims=True)
    gsel = jnp.min(jnp.where(gl == gmax, sub, N_EXPERTS), axis=0, keepdims=True)
    g_w = 1.0 / jnp.sum(jnp.exp(gl - gmax), axis=0, keepdims=True)
    el = lg[0:E_PER_GROUP, :]
    for gi in range(1, N_GROUPS):
        el = jnp.where(gsel == gi, lg[gi * E_PER_GROUP:(gi + 1) * E_PER_GROUP, :], el)
    l1 = jnp.max(el, axis=0, keepdims=True)
    i1 = jnp.min(jnp.where(el == l1, sub, N_EXPERTS), axis=0, keepdims=True)
    el2 = jnp.where(sub == i1, NEG, el)
    l2 = jnp.max(el2, axis=0, keepdims=True)
    i2 = jnp.min(jnp.where(el2 == l2, sub, N_EXPERTS), axis=0, keepdims=True)
    p2 = jnp.exp(l2 - l1)
    w1 = g_w / (1.0 + p2)
    w2 = g_w * p2 / (1.0 + p2)
    e1 = gsel * E_PER_GROUP + i1
    e2 = gsel * E_PER_GROUP + i2

    erow = lax.broadcasted_iota(I32, (N_EXPERTS, tm), 0)
    oh = ((erow == e1) | (erow == e2))
    tr = lax.broadcasted_iota(I32, (tm, tm), 0)
    tc = lax.broadcasted_iota(I32, (tm, tm), 1)
    before = (tr < tc).astype(BF16)
    cum = _dot(oh.astype(BF16), before) + cnt_sc[:, 0:1]
    r1 = jnp.sum(jnp.where(erow == e1, cum, 0.0), axis=0, keepdims=True)
    r2 = jnp.sum(jnp.where(erow == e2, cum, 0.0), axis=0, keepdims=True)
    cnt_sc[...] = cnt_sc[...] + jnp.sum(oh.astype(F32), axis=1, keepdims=True)

    ri_ref[...] = jnp.where(sub == 0, e1, jnp.where(sub == 1, e2, jnp.where(
        sub == 2, r1.astype(I32), jnp.where(sub == 3, r2.astype(I32), 0))))
    rw_ref[...] = jnp.where(sub == 0, w1, jnp.where(sub == 1, w2, 0.0))
    cnt_ref[...] = cnt_sc[...]


def _router(x, g, w_rt, b_rt, base_cnt, tm):
    n = x.shape[0]
    return pl.pallas_call(
        _router_kernel,
        grid=(n // tm,),
        in_specs=[pl.BlockSpec((tm, D_MODEL), lambda i: (i, 0)), _full((1, D_MODEL)),
                  _full((LANES, D_MODEL)), _full((LANES, LANES)), _full((N_EXPERTS, LANES))],
        out_specs=[pl.BlockSpec((tm, D_MODEL), lambda i: (i, 0)),
                   pl.BlockSpec((SUBLANES, tm), lambda i: (0, i)),
                   pl.BlockSpec((SUBLANES, tm), lambda i: (0, i)),
                   _full((N_EXPERTS, LANES))],
        out_shape=[jax.ShapeDtypeStruct((n, D_MODEL), F32),
                   jax.ShapeDtypeStruct((SUBLANES, n), I32),
                   jax.ShapeDtypeStruct((SUBLANES, n), F32),
                   jax.ShapeDtypeStruct((N_EXPERTS, LANES), F32)],
        scratch_shapes=[pltpu.VMEM((N_EXPERTS, LANES), F32)],
        compiler_params=_cparams("arbitrary"),
        name="moe_router",
    )(x, g, w_rt, b_rt, base_cnt)


def _dispatch_kernel(pos_ref, xn_ref, xs_in_ref, xs_ref, sem):
    del xs_in_ref
    tm = xn_ref.shape[0]

    def row_copy(t, j):
        return pltpu.make_async_copy(xn_ref.at[pl.ds(t, 1), :],
                                     xs_ref.at[pl.ds(pos_ref[0, 0, j * tm + t], 1), :], sem)

    def issue(t, c):
        row_copy(t, 0).start()
        row_copy(t, 1).start()
        return c

    def drain(t, c):
        row_copy(t, 0).wait()
        row_copy(t, 1).wait()
        return c

    lax.fori_loop(0, tm, issue, 0)
    lax.fori_loop(0, tm, drain, 0)


def _dispatch(pos3, xn, xs, tm):
    n = xn.shape[0]
    return pl.pallas_call(
        _dispatch_kernel,
        grid=(n // tm,),
        in_specs=[pl.BlockSpec((1, 1, 2 * tm), lambda i: (i, 0, 0), memory_space=pltpu.SMEM),
                  pl.BlockSpec((tm, D_MODEL), lambda i: (i, 0)),
                  pl.BlockSpec(memory_space=pl.ANY)],
        out_specs=pl.BlockSpec(memory_space=pl.ANY),
        out_shape=jax.ShapeDtypeStruct(xs.shape, xs.dtype),
        scratch_shapes=[pltpu.SemaphoreType.DMA(())],
        input_output_aliases={2: 0},
        compiler_params=_cparams("arbitrary"),
        name="moe_dispatch",
    )(pos3, xn, xs)


def _experts_kernel(te_ref, nt_ref, xs_ref, wgu_ref, wd_ref, o_ref):
    i = pl.program_id(0)

    @pl.when(i < nt_ref[0])
    def _run():
        x = xs_ref[...].astype(BF16)
        gu = _dot(x, wgu_ref[...])
        gate = gu[:, 0:D_FF]
        h = gate * jax.nn.sigmoid(gate) * gu[:, D_FF:2 * D_FF]
        o_ref[...] = _dot(h.astype(BF16), wd_ref[...])

    @pl.when(i >= nt_ref[0])
    def _unused_tile():
        o_ref[...] = jnp.zeros(o_ref.shape, o_ref.dtype)


def _experts(tile_expert, n_tiles, xs, w_gu, w_d, te):
    max_tiles = xs.shape[0] // te
    clamp = lambda i, nt: jnp.minimum(i, nt[0] - 1)
    rows = lambda i, te_ref, nt: (clamp(i, nt), 0)
    wsel = lambda i, te_ref, nt: (te_ref[clamp(i, nt)], 0, 0)
    grid_spec = pltpu.PrefetchScalarGridSpec(
        num_scalar_prefetch=2,
        grid=(max_tiles,),
        in_specs=[pl.BlockSpec((te, D_MODEL), rows),
                  pl.BlockSpec((None, D_MODEL, 2 * D_FF), wsel),
                  pl.BlockSpec((None, D_FF, D_MODEL), wsel)],
        out_specs=pl.BlockSpec((te, D_MODEL), lambda i, te_ref, nt: (i, 0)),
    )
    return pl.pallas_call(
        _experts_kernel,
        grid_spec=grid_spec,
        out_shape=jax.ShapeDtypeStruct(xs.shape, F32),
        compiler_params=_cparams("arbitrary"),
        name="moe_experts",
    )(tile_expert, n_tiles, xs, w_gu, w_d)


def _combine_kernel(pos_ref, x_ref, rw_ref, g_ref, ys_ref, o_ref, buf, sem, *, final_norm):
    tm = x_ref.shape[0]

    def row_copy(t, j):
        return pltpu.make_async_copy(ys_ref.at[pl.ds(pos_ref[0, 0, j * tm + t], 1), :],
                                     buf.at[j, pl.ds(t, 1), :], sem)

    def issue(t, c):
        row_copy(t, 0).start()
        row_copy(t, 1).start()
        return c

    def drain(t, c):
        row_copy(t, 0).wait()
        row_copy(t, 1).wait()
        return c

    lax.fori_loop(0, tm, issue, 0)
    wt = jnp.concatenate([rw_ref[...], jnp.zeros((LANES - SUBLANES, tm), F32)], axis=0).T
    lax.fori_loop(0, tm, drain, 0)
    y = x_ref[...] + wt[:, 0:1] * buf[0] + wt[:, 1:2] * buf[1]
    if final_norm:
        y = _rms(y, g_ref[...])
    o_ref[...] = y


def _combine(pos3, x, rw, g, ys, tm, final_norm):
    n = x.shape[0]
    return pl.pallas_call(
        functools.partial(_combine_kernel, final_norm=final_norm),
        grid=(n // tm,),
        in_specs=[pl.BlockSpec((1, 1, 2 * tm), lambda i: (i, 0, 0), memory_space=pltpu.SMEM),
                  pl.BlockSpec((tm, D_MODEL), lambda i: (i, 0)),
                  pl.BlockSpec((SUBLANES, tm), lambda i: (0, i)),
                  _full((1, D_MODEL)),
                  pl.BlockSpec(memory_space=pl.ANY)],
        out_specs=pl.BlockSpec((tm, D_MODEL), lambda i: (i, 0)),
        out_shape=jax.ShapeDtypeStruct((n, D_MODEL), F32),
        scratch_shapes=[pltpu.VMEM((2, tm, D_MODEL), F32), pltpu.SemaphoreType.DMA(())],
        compiler_params=_cparams("arbitrary"),
        name="moe_combine",
    )(pos3, x, rw, g, ys)


def _moe(xs_groups, g, w_rt, b_rt, w_gu, w_d, xs_buf, final_g):
    tm, te = MOE_TM, MOE_TE
    base = jnp.zeros((N_EXPERTS, LANES), F32)
    routed = []
    for x in xs_groups:
        xn, ri, rw, base = _router(x, g, w_rt, b_rt, base, tm)
        routed.append((xn, ri, rw))
    counts = base[:, 0].astype(I32)
    padded = ((counts + te - 1) // te) * te
    ends = jnp.cumsum(padded)
    starts = ends - padded
    max_tiles = xs_buf.shape[0] // te
    tile_expert = jnp.minimum(jnp.searchsorted(ends // te, jnp.arange(max_tiles, dtype=I32), side="right"),
                              N_EXPERTS - 1).astype(I32)
    n_tiles = (ends[-1:] // te).astype(I32)
    pos3s = []
    for xn, ri, rw in routed:
        n = xn.shape[0]
        pos = starts[ri[0:2]] + ri[2:4]
        pos3 = pos.reshape(2, n // tm, tm).transpose(1, 0, 2).reshape(n // tm, 1, 2 * tm)
        pos3s.append(pos3)
        xs_buf = _dispatch(pos3, xn, xs_buf, tm)
    ys = _experts(tile_expert, n_tiles, xs_buf, w_gu, w_d, te)
    outs = []
    for x, pos3, (xn, ri, rw) in zip(xs_groups, pos3s, routed):
        gg = final_g if final_g is not None else g
        outs.append(_combine(pos3, x, rw, gg, ys, tm, final_g is not None))
    return outs, xs_buf


def _rope_tables(pos):
    half = DH_C // 2
    inv_freq = jnp.exp(-math.log(ROPE_THETA) * jnp.arange(half, dtype=F32) / half)
    ang = pos.astype(F32)[:, None] * inv_freq[None, :]
    cos = jnp.cos(ang)
    sin = jnp.sin(ang)
    cos_t = jnp.tile(jnp.concatenate([cos, cos], axis=1), (1, LANES // DH_C))
    sin_t = jnp.tile(jnp.concatenate([-sin, sin], axis=1), (1, LANES // DH_C))
    return cos_t, sin_t


def kernel(x_prompt, x_sample, state_mlstm_C, state_mlstm_n, state_mlstm_m, state_conv, cache_k, cache_v, page_table, cache_mem_k, cache_mem_v, mem_prompt, norm_mix_g, norm_mem_g, norm_ffn_g, final_norm_g, w_in_a, gate_b_a, head_norm_g_a, conv_w_b, w_out_a, w_qkv_c, lambda_c, subln_g_c, w_out_c, w_mem_q, w_mem_k, w_mem_v, w_mem_o, w_router_g, b_router_g, w_router_e, b_router_e, w_exp_gate, w_exp_up, w_exp_down):
    bp, tp, d = x_prompt.shape
    bs, ts, _ = x_sample.shape
    depth = norm_mix_g.shape[0]
    n_p, n_s = bp * tp, bs * ts
    past_len = page_table.shape[1] * PAGE_SIZE
    assert d == D_MODEL and ts == SUBLANES
    tm_p = min(ROW_TILE, n_p)
    tm_s = min(ROW_TILE, n_s)

    xp = x_prompt.reshape(n_p, d)
    xs = x_sample.reshape(n_s, d)
    row = lambda v: v.reshape(1, -1).astype(F32)

    mem_rows = mem_prompt.reshape(bp * N_MEM, d)
    tm_m = min(ROW_TILE, bp * N_MEM)
    p_mem_k = jnp.stack([_mm(mem_rows, w_mem_k[l].astype(BF16), F32, tm_m) for l in range(depth)])
    p_mem_v = jnp.stack([_mm(mem_rows, w_mem_v[l].astype(BF16), F32, tm_m) for l in range(depth)])
    p_mem_k = p_mem_k.reshape(depth, bp, N_MEM, d)
    p_mem_v = p_mem_v.reshape(depth, bp, N_MEM, d)
    s_mem_k = cache_mem_k.reshape(depth, bs, N_MEM, d)
    s_mem_v = cache_mem_v.reshape(depth, bs, N_MEM, d)

    n_all = n_p + n_s
    xs_buf = jnp.zeros((2 * n_all + N_EXPERTS * MOE_TE, d), F32)

    out_c_p, out_n_p, out_m_p, out_conv_p, out_k_p, out_v_p = [], [], [], [], [], []
    out_c_s, out_n_s, out_m_s, out_conv_s, out_k_s, out_v_s = [], [], [], [], [], []
    ia = ic = 0
    for layer in range(depth):
        g_mix = row(norm_mix_g[layer])
        if layer % 2 == 0:
            w = w_in_a[ia]
            k_scale = DH_A ** -0.5
            w_main = jnp.concatenate([w[:, 0:D_A], w[:, D_A:2 * D_A] * k_scale, w[:, 2 * D_A:4 * D_A],
                                      w[:, 4 * D_A + 2 * H_A:]], axis=1).astype(BF16)
            w_gate = jnp.pad(w[:, 4 * D_A:4 * D_A + 2 * H_A], ((0, 0), (0, LANES - 2 * H_A))).astype(BF16)
            gbias = jnp.pad(gate_b_a[ia].reshape(1, 2 * H_A).astype(F32), ((0, 0), (0, LANES - 2 * H_A)))
            head_g = row(head_norm_g_a[ia])
            conv_w = jnp.pad(conv_w_b[ia].astype(F32), ((0, SUBLANES - CONV_W), (0, 0)))
            w_out = w_out_a[ia].astype(BF16)

            qkvo, cv, gt = _inproj(xp, g_mix, w_main, w_gate, BF16, tm_p)
            zeros = lambda *s: jnp.zeros(s, F32)
            y, c_n, n_n, m_n, cb_n = _mlstm(
                qkvo, cv, gt, gbias, head_g, conv_w,
                zeros(bp, H_A, DH_A, DH_A), zeros(bp, SUBLANES, DH_A), zeros(bp, SUBLANES, LANES),
                zeros(bp, SUBLANES, D_B), min(MLSTM_L, tp), BF16)
            xp = _mm_res(xp, y, w_out, tm_p)
            out_c_p.append(c_n)
            out_n_p.append(n_n[:, 0:H_A])
            out_m_p.append(m_n[:, 0:H_A, 0])
            out_conv_p.append(cb_n[:, SUBLANES - (CONV_W - 1):])

            qkvo, cv, gt = _inproj(xs, g_mix, w_main, w_gate, F32, tm_s)
            n0 = jnp.pad(state_mlstm_n[ia].astype(F32), ((0, 0), (0, SUBLANES - H_A), (0, 0)))
            m0 = jnp.pad(jnp.broadcast_to(state_mlstm_m[ia].astype(F32)[:, :, None], (bs, H_A, LANES)),
                         ((0, 0), (0, SUBLANES - H_A), (0, 0)))
            cb0 = jnp.pad(state_conv[ia].astype(F32), ((0, 0), (SUBLANES - (CONV_W - 1), 0), (0, 0)))
            y, c_n, n_n, m_n, cb_n = _mlstm(
                qkvo, cv, gt, gbias, head_g, conv_w,
                state_mlstm_C[ia].astype(F32), n0, m0, cb0, ts, F32)
            xs = _mm_res(xs, y, w_out, tm_s)
            out_c_s.append(c_n)
            out_n_s.append(n_n[:, 0:H_A])
            out_m_s.append(m_n[:, 0:H_A, 0])
            out_conv_s.append(cb_n[:, SUBLANES - (CONV_W - 1):])
            ia += 1
        else:
            w_qkv = w_qkv_c[ic].astype(BF16)
            w_out = w_out_c[ic].astype(BF16)
            subln_g = row(subln_g_c[ic])
            lam_init = 0.8 - 0.6 * math.exp(-0.3 * layer)
            lv = lambda_c[ic].astype(F32)
            lam = (jnp.exp(jnp.sum(lv[0] * lv[1])) - jnp.exp(jnp.sum(lv[2] * lv[3])) + lam_init).reshape(1)
            out_scale = 1.0 - lam_init

            cos_p, sin_p = _rope_tables(jnp.arange(tp))
            npt = tp // min(tm_p, tp)
            tmq = min(tm_p, tp)
            q, k, v = _qkv_rope(xp, g_mix, w_qkv, cos_p, sin_p, lambda i: (i % npt, 0), BF16, tmq)
            o = _flash_diff(lam, q, k, v, subln_g, bp, out_scale, BF16)
            xp = _mm_res(xp, o, w_out, tm_p)
            out_k_p.append(k)
            out_v_p.append(v)

            cos_s, sin_s = _rope_tables(past_len + jnp.arange(ts))
            cos_s = jnp.tile(cos_s, (tm_s // ts, 1))
            sin_s = jnp.tile(sin_s, (tm_s // ts, 1))
            q, k, v = _qkv_rope(xs, g_mix, w_qkv, cos_s, sin_s, lambda i: (0, 0), F32, tm_s)
            n_pool = cache_k.shape[1]
            o = _paged_diff(page_table, lam, q, cache_k.reshape(-1, n_pool, PAGE_SIZE, D_C),
                            cache_v.reshape(-1, n_pool, PAGE_SIZE, D_C), k, v, subln_g, ic, ts, out_scale)
            xs = _mm_res(xs, o, w_out, tm_s)
            out_k_s.append(k)
            out_v_s.append(v)
            ic += 1

        g_mem = row(norm_mem_g[layer])
        w_q = w_mem_q[layer].astype(BF16)
        w_o = w_mem_o[layer].astype(BF16)
        q = _norm_mm(xp, g_mem, w_q, BF16, tm_p, scale=DH_M ** -0.5)
        o = _mem_attn(q, p_mem_k[layer], p_mem_v[layer], min(MEM_TQ, tp), BF16)
        xp = _mm_res(xp, o, w_o, tm_p)
        q = _norm_mm(xs, g_mem, w_q, F32, tm_s, scale=DH_M ** -0.5)
        o = _mem_attn(q, s_mem_k[layer], s_mem_v[layer], ts, F32)
        xs = _mm_res(xs, o, w_o, tm_s)

        g_ffn = row(norm_ffn_g[layer])
        w_rt = jnp.zeros((LANES, d), F32)
        w_rt = w_rt.at[0:N_EXPERTS].set(w_router_e[layer].astype(F32).T)
        w_rt = w_rt.at[N_EXPERTS:N_EXPERTS + N_GROUPS].set(w_router_g[layer].astype(F32).T)
        b_col = jnp.zeros((LANES,), F32).at[0:N_EXPERTS].set(b_router_e[layer].astype(F32))
        b_col = b_col.at[N_EXPERTS:N_EXPERTS + N_GROUPS].set(b_router_g[layer].astype(F32))
        b_col = b_col.at[N_EXPERTS + N_GROUPS:N_EXPERTS + SUBLANES].set(NEG)
        b_rt = jnp.broadcast_to(b_col[:, None], (LANES, LANES))
        w_gu = jnp.concatenate([w_exp_gate[layer], w_exp_up[layer]], axis=-1).reshape(
            N_EXPERTS, d, 2 * D_FF).astype(BF16)
        w_d = w_exp_down[layer].reshape(N_EXPERTS, D_FF, d).astype(BF16)
        final_g = row(final_norm_g) if layer == depth - 1 else None
        (xp, xs), xs_buf = _moe([xp, xs], g_ffn, w_rt, b_rt, w_gu, w_d, xs_buf, final_g)

    y_prompt = xp.reshape(bp, tp, d)
    y_sample = xs.reshape(bs, ts, d)
    stk = jnp.stack
    return (y_prompt, y_sample,
            stk(out_c_p), stk(out_n_p), stk(out_m_p), stk(out_conv_p),
            stk(out_k_p).reshape(-1, bp, tp, H_C, 2, DH_C), stk(out_v_p).reshape(-1, bp, tp, H_C, 2 * DH_C),
            p_mem_k.reshape(depth, bp, N_MEM, H_M, DH_M), p_mem_v.reshape(depth, bp, N_MEM, H_M, DH_M),
            stk(out_c_s), stk(out_n_s), stk(out_m_s), stk(out_conv_s),
            stk(out_k_s).reshape(-1, bs, ts, H_C, 2, DH_C), stk(out_v_s).reshape(-1, bs, ts, H_C, 2 * DH_C))
```

```python
import functools
import math

import jax
import jax.numpy as jnp
from jax import lax
from jax.experimental import pallas as pl
from jax.experimental.pallas import tpu as pltpu

F32 = jnp.float32
BF16 = jnp.bfloat16
I32 = jnp.int32

EPS = 1e-6
NEG = -1e30
ROPE_THETA = 10000.0
PAGE_SIZE = 128
CONV_W = 3

D_MODEL = 1024
H_A = 4
DH_A = 128
D_A = H_A * DH_A
D_B = 512
H_C = 8
DH_C = 64
D_C = 2 * H_C * DH_C
N_MEM = 256
H_M = 4
DH_M = D_MODEL // H_M
N_GROUPS = 4
E_PER_GROUP = 8
N_EXPERTS = N_GROUPS * E_PER_GROUP
D_FF = D_MODEL // 4

LANES = 128
SUBLANES = 8
VMEM_LIMIT = 56 * 1024 * 1024

ROW_TILE = 512
MLSTM_L = 256
ATT_TQ = 512
MEM_TQ = 512
MOE_TM = 256
MOE_TE = 256


def _cparams(*sem):
    return pltpu.CompilerParams(dimension_semantics=sem, vmem_limit_bytes=VMEM_LIMIT)


def _rms(x, g):
    return x * lax.rsqrt(jnp.mean(x * x, axis=-1, keepdims=True) + EPS) * g


def _dot(a, b):
    return jnp.dot(a, b, preferred_element_type=F32)


def _dot_nt(a, b):
    return lax.dot_general(a, b, (((1,), (1,)), ((), ())), preferred_element_type=F32)


def _dot_tn(a, b):
    return lax.dot_general(a, b, (((0,), (0,)), ((), ())), preferred_element_type=F32)


def _split3(x):
    hi = x.astype(BF16)
    r1 = x - hi.astype(F32)
    mid = r1.astype(BF16)
    lo = (r1 - mid.astype(F32)).astype(BF16)
    return hi, mid, lo


def _full(shape):
    n = len(shape)
    return pl.BlockSpec(shape, lambda *_: (0,) * n)


def _inproj_kernel(x_ref, g_ref, w_ref, wg_ref, qkvo_ref, conv_ref, gate_ref):
    xn = _rms(x_ref[...], g_ref[...]).astype(BF16)
    for c in range(4 * D_A // 512):
        qkvo_ref[:, c * 512:(c + 1) * 512] = _dot(xn, w_ref[:, c * 512:(c + 1) * 512]).astype(qkvo_ref.dtype)
    off = 4 * D_A
    for c in range(3 * D_B // 512):
        conv_ref[:, c * 512:(c + 1) * 512] = _dot(
            xn, w_ref[:, off + c * 512:off + (c + 1) * 512]).astype(conv_ref.dtype)
    gate_ref[...] = _dot(xn, wg_ref[...])


def _inproj(x, g, w_main, w_gate, out_dtype, tm):
    n = x.shape[0]
    return pl.pallas_call(
        _inproj_kernel,
        grid=(n // tm,),
        in_specs=[pl.BlockSpec((tm, D_MODEL), lambda i: (i, 0)), _full((1, D_MODEL)),
                  _full(w_main.shape), _full(w_gate.shape)],
        out_specs=[pl.BlockSpec((tm, 4 * D_A), lambda i: (i, 0)),
                   pl.BlockSpec((tm, 3 * D_B), lambda i: (i, 0)),
                   pl.BlockSpec((tm, LANES), lambda i: (i, 0))],
        out_shape=[jax.ShapeDtypeStruct((n, 4 * D_A), out_dtype),
                   jax.ShapeDtypeStruct((n, 3 * D_B), out_dtype),
                   jax.ShapeDtypeStruct((n, LANES), F32)],
        compiler_params=_cparams("parallel"),
        name="inproj",
    )(x, g, w_main, w_gate)


def _norm_mm_kernel(x_ref, g_ref, w_ref, o_ref, *, scale):
    xn = _rms(x_ref[...], g_ref[...]).astype(BF16)
    o_ref[...] = (_dot(xn, w_ref[...]) * scale).astype(o_ref.dtype)


def _norm_mm(x, g, w, out_dtype, tm, scale=1.0):
    n = x.shape[0]
    nout = w.shape[1]
    return pl.pallas_call(
        functools.partial(_norm_mm_kernel, scale=scale),
        grid=(n // tm,),
        in_specs=[pl.BlockSpec((tm, D_MODEL), lambda i: (i, 0)), _full((1, D_MODEL)), _full(w.shape)],
        out_specs=pl.BlockSpec((tm, nout), lambda i: (i, 0)),
        out_shape=jax.ShapeDtypeStruct((n, nout), out_dtype),
        compiler_params=_cparams("parallel"),
        name="norm_mm",
    )(x, g, w)


def _mm_kernel(a_ref, w_ref, o_ref):
    o_ref[...] = _dot(a_ref[...].astype(BF16), w_ref[...]).astype(o_ref.dtype)


def _mm(a, w, out_dtype, tm):
    n, k = a.shape
    nout = w.shape[1]
    return pl.pallas_call(
        _mm_kernel,
        grid=(n // tm,),
        in_specs=[pl.BlockSpec((tm, k), lambda i: (i, 0)), _full(w.shape)],
        out_specs=pl.BlockSpec((tm, nout), lambda i: (i, 0)),
        out_shape=jax.ShapeDtypeStruct((n, nout), out_dtype),
        compiler_params=_cparams("parallel"),
        name="mm",
    )(a, w)


def _mm_res_kernel(x_ref, a_ref, w_ref, o_ref):
    o_ref[...] = x_ref[...] + _dot(a_ref[...].astype(BF16), w_ref[...])


def _mm_res(x, a, w, tm):
    n, k = a.shape
    return pl.pallas_call(
        _mm_res_kernel,
        grid=(n // tm,),
        in_specs=[pl.BlockSpec((tm, D_MODEL), lambda i: (i, 0)),
                  pl.BlockSpec((tm, k), lambda i: (i, 0)), _full(w.shape)],
        out_specs=pl.BlockSpec((tm, D_MODEL), lambda i: (i, 0)),
        out_shape=jax.ShapeDtypeStruct((n, D_MODEL), F32),
        compiler_params=_cparams("parallel"),
        name="mm_res",
    )(x, a, w)


def _qkv_rope_kernel(x_ref, g_ref, w_ref, cos_ref, sin_ref, q_ref, k_ref, v_ref, *, q_scale):
    xn = _rms(x_ref[...], g_ref[...]).astype(BF16)
    reps = D_C // LANES
    cos = jnp.concatenate([cos_ref[...]] * reps, axis=1)
    sin = jnp.concatenate([sin_ref[...]] * reps, axis=1)
    tm = xn.shape[0]
    lane = lax.broadcasted_iota(I32, (tm, D_C), 1)
    first_half = (lane % DH_C) < (DH_C // 2)

    def rope(t):
        partner = jnp.where(first_half, pltpu.roll(t, D_C - DH_C // 2, axis=1), pltpu.roll(t, DH_C // 2, axis=1))
        return t * cos + partner * sin

    q = rope(_dot(xn, w_ref[:, 0:D_C]))
    q_ref[...] = (q * q_scale).astype(q_ref.dtype)
    k_ref[...] = rope(_dot(xn, w_ref[:, D_C:2 * D_C]))
    v_ref[...] = _dot(xn, w_ref[:, 2 * D_C:3 * D_C])


def _qkv_rope(x, g, w, cos, sin, tab_map, q_dtype, tm):
    n = x.shape[0]
    row = lambda i: (i, 0)
    return pl.pallas_call(
        functools.partial(_qkv_rope_kernel, q_scale=DH_C ** -0.5 * math.log2(math.e)),
        grid=(n // tm,),
        in_specs=[pl.BlockSpec((tm, D_MODEL), row), _full((1, D_MODEL)), _full(w.shape),
                  pl.BlockSpec((tm, LANES), tab_map), pl.BlockSpec((tm, LANES), tab_map)],
        out_specs=[pl.BlockSpec((tm, D_C), row)] * 3,
        out_shape=[jax.ShapeDtypeStruct((n, D_C), q_dtype),
                   jax.ShapeDtypeStruct((n, D_C), F32),
                   jax.ShapeDtypeStruct((n, D_C), F32)],
        compiler_params=_cparams("parallel"),
        name="qkv_rope",
    )(x, g, w, cos, sin)


def _log_sigmoid(x):
    return jnp.minimum(x, 0.0) - jnp.log(1.0 + jnp.exp(-jnp.abs(x)))


def _mlstm_kernel(qkvo_ref, conv_ref, gate_ref, gbias_ref, hg_ref, cw_ref,
                  c0_ref, n0_ref, m0_ref, cb0_ref,
                  y_ref, c_out, n_out, m_out, cb_out,
                  c_sc, n_sc, m_sc, cb_sc, *, L, nchunks):
    ci = pl.program_id(1)

    @pl.when(ci == 0)
    def _load_state():
        c_sc[...] = c0_ref[...]
        n_sc[...] = n0_ref[...]
        m_sc[...] = m0_ref[...]
        cb_sc[...] = cb0_ref[...]

    mm_dtype = BF16 if L % 16 == 0 else F32

    gates = gate_ref[...] + gbias_ref[...]
    lf = _log_sigmoid(gates)
    row = lax.broadcasted_iota(I32, (L, L), 0)
    col = lax.broadcasted_iota(I32, (L, L), 1)
    tri = row >= col
    tri_b = tri.astype(mm_dtype)
    bc = sum(_dot(tri_b, t.astype(mm_dtype)) for t in _split3(lf))
    dmat = gates - pltpu.roll(bc, LANES - H_A, axis=1)
    sel = (lax.broadcasted_iota(I32, (SUBLANES, LANES), 0)
           == lax.broadcasted_iota(I32, (SUBLANES, LANES), 1)).astype(mm_dtype)
    dt = sum(_dot_nt(sel, t.astype(mm_dtype)) for t in _split3(dmat))

    for h in range(H_A):
        bcol = bc[:, H_A + h:H_A + h + 1]
        igcol = gates[:, h:h + 1]
        rowv = dt[h:h + 1, :]
        m0 = m_sc[h:h + 1, 0:1]
        logd = jnp.where(tri, bcol + rowv, NEG)
        log_inter = bcol + m0
        m_t = jnp.maximum(log_inter, jnp.max(logd, axis=-1, keepdims=True))
        d = jnp.exp(logd - m_t)
        inter = jnp.exp(log_inter - m_t)

        q = qkvo_ref[:, h * DH_A:(h + 1) * DH_A]
        k = qkvo_ref[:, D_A + h * DH_A:D_A + (h + 1) * DH_A]
        v = qkvo_ref[:, 2 * D_A + h * DH_A:2 * D_A + (h + 1) * DH_A]
        o = qkvo_ref[:, 3 * D_A + h * DH_A:3 * D_A + (h + 1) * DH_A]
        qm, km, vm = q.astype(mm_dtype), k.astype(mm_dtype), v.astype(mm_dtype)
        qf, kf, vf = q.astype(F32), k.astype(F32), v.astype(F32)

        s = _dot_nt(qm, km) * d
        c0 = c_sc[h]
        n0 = n_sc[h:h + 1, :]
        num = _dot(s.astype(mm_dtype), vm) + inter * _dot_nt(qm, c0.astype(mm_dtype))
        den = jnp.sum(s, axis=-1, keepdims=True) + inter * jnp.sum(qf * n0, axis=-1, keepdims=True)
        hh = num / jnp.maximum(jnp.abs(den), jnp.exp(-m_t))

        b_last = bcol[L - 1:L, :]
        m_last = m_t[L - 1:L, :]
        wcol = jnp.exp(b_last - bcol + igcol - m_last)
        decay = inter[L - 1:L, :]
        c_sc[h] = decay * c0 + _dot_tn((vf * wcol).astype(mm_dtype), km)
        n_sc[h:h + 1, :] = decay * n0 + jnp.sum(wcol * kf, axis=0, keepdims=True)
        m_sc[h:h + 1, :] = jnp.broadcast_to(m_last, (1, LANES))

        hn = hh * lax.rsqrt(jnp.mean(hh * hh, axis=-1, keepdims=True) + EPS) * hg_ref[:, h * DH_A:(h + 1) * DH_A]
        y_ref[:, h * DH_A:(h + 1) * DH_A] = (jax.nn.sigmoid(o.astype(F32)) * hn).astype(y_ref.dtype)

    gbv = conv_ref[:, 0:D_B].astype(F32)
    u = conv_ref[:, D_B:2 * D_B].astype(F32) * conv_ref[:, 2 * D_B:3 * D_B].astype(F32)
    carry = cb_sc[...]
    rowi = lax.broadcasted_iota(I32, (L, D_B), 0)
    prev1 = carry[SUBLANES - 1:SUBLANES, :]
    prev2 = carry[SUBLANES - 2:SUBLANES - 1, :]
    um1 = jnp.where(rowi == 0, prev1, pltpu.roll(u, 1, axis=0))
    um2 = jnp.where(rowi == 0, prev2, jnp.where(rowi == 1, prev1, pltpu.roll(u, 2, axis=0)))
    conv = um2 * cw_ref[0:1, :] + um1 * cw_ref[1:2, :] + u * cw_ref[2:3, :]
    y_ref[:, D_A:D_A + D_B] = (gbv * conv).astype(y_ref.dtype)
    cb_sc[...] = u[L - SUBLANES:L, :]

    @pl.when(ci == nchunks - 1)
    def _store_state():
        c_out[...] = c_sc[...]
        n_out[...] = n_sc[...]
        m_out[...] = m_sc[...]
        cb_out[...] = cb_sc[...]


def _mlstm(qkvo, conv, gates, gbias, head_g, conv_w, c0, n0, m0, cb0, L, out_dtype):
    nb = c0.shape[0]
    t = qkvo.shape[0] // nb
    nchunks = t // L
    rows = lambda b, c: (b * nchunks + c, 0)
    st4 = lambda b, c: (b, 0, 0, 0)
    st3 = lambda b, c: (b, 0, 0)
    return pl.pallas_call(
        functools.partial(_mlstm_kernel, L=L, nchunks=nchunks),
        grid=(nb, nchunks),
        in_specs=[pl.BlockSpec((L, 4 * D_A), rows), pl.BlockSpec((L, 3 * D_B), rows),
                  pl.BlockSpec((L, LANES), rows),
                  _full((1, LANES)), _full((1, D_A)), _full((SUBLANES, D_B)),
                  pl.BlockSpec((None, H_A, DH_A, DH_A), st4),
                  pl.BlockSpec((None, SUBLANES, DH_A), st3),
                  pl.BlockSpec((None, SUBLANES, LANES), st3),
                  pl.BlockSpec((None, SUBLANES, D_B), st3)],
        out_specs=[pl.BlockSpec((L, D_A + D_B), rows),
                   pl.BlockSpec((None, H_A, DH_A, DH_A), st4),
                   pl.BlockSpec((None, SUBLANES, DH_A), st3),
                   pl.BlockSpec((None, SUBLANES, LANES), st3),
                   pl.BlockSpec((None, SUBLANES, D_B), st3)],
        out_shape=[jax.ShapeDtypeStruct((nb * t, D_A + D_B), out_dtype),
                   jax.ShapeDtypeStruct((nb, H_A, DH_A, DH_A), F32),
                   jax.ShapeDtypeStruct((nb, SUBLANES, DH_A), F32),
                   jax.ShapeDtypeStruct((nb, SUBLANES, LANES), F32),
                   jax.ShapeDtypeStruct((nb, SUBLANES, D_B), F32)],
        scratch_shapes=[pltpu.VMEM((H_A, DH_A, DH_A), F32), pltpu.VMEM((SUBLANES, DH_A), F32),
                        pltpu.VMEM((SUBLANES, LANES), F32), pltpu.VMEM((SUBLANES, D_B), F32)],
        compiler_params=_cparams("parallel", "arbitrary"),
        name="mlstm_conv",
    )(qkvo, conv, gates, gbias, head_g, conv_w, c0, n0, m0, cb0)


def _mem_attn_kernel(q_ref, *refs, head_major):
    n_kv = H_M if head_major else 1
    k_refs, v_refs, o_ref = refs[0:n_kv], refs[n_kv:2 * n_kv], refs[2 * n_kv]
    mm_dtype = BF16 if q_ref.shape[0] % 16 == 0 else F32
    for h in range(H_M):
        sl = slice(h * DH_M, (h + 1) * DH_M)
        q = q_ref[:, sl].astype(mm_dtype)
        if head_major:
            k = k_refs[h][...].astype(mm_dtype)
            v = v_refs[h][...].astype(mm_dtype)
        else:
            k = k_refs[0][:, h, :].astype(mm_dtype)
            v = v_refs[0][:, h, :].astype(mm_dtype)
        s = _dot_nt(q, k)
        p = jnp.exp2(s - jnp.max(s, axis=-1, keepdims=True))
        p = p / jnp.sum(p, axis=-1, keepdims=True)
        o_ref[:, sl] = _dot(p.astype(mm_dtype), v).astype(o_ref.dtype)


def _mem_attn(q, mem_k, mem_v, kv_specs, nb, tq, out_dtype):
    nq = q.shape[0] // nb // tq
    rows = lambda b, i: (b * nq + i, 0)
    n_kv = len(kv_specs)
    return pl.pallas_call(
        functools.partial(_mem_attn_kernel, head_major=n_kv == H_M),
        grid=(nb, nq),
        in_specs=[pl.BlockSpec((tq, D_MODEL), rows)] + kv_specs + kv_specs,
        out_specs=pl.BlockSpec((tq, D_MODEL), rows),
        out_shape=jax.ShapeDtypeStruct((q.shape[0], D_MODEL), out_dtype),
        compiler_params=_cparams("parallel", "arbitrary"),
        name="mem_attn",
    )(q, *([mem_k] * n_kv), *([mem_v] * n_kv))


def _flash_diff_kernel(qi_ref, ki_ref, lam_ref, q_ref, k_ref, v_ref, g_ref, o_ref,
                       qx_sc, m_sc, l_sc, acc_sc, *, tq, out_scale):
    pi = pl.program_id(2)
    qi = qi_ref[pi]
    ki = ki_ref[pi]

    @pl.when(ki == 0)
    def _init():
        q = q_ref[...]
        lane = lax.broadcasted_iota(I32, q.shape, 1)
        zero = jnp.zeros_like(q)
        qx_sc[0:tq, :] = jnp.where(lane < DH_C, q, zero)
        qx_sc[tq:2 * tq, :] = jnp.where(lane >= DH_C, q, zero)
        m_sc[...] = jnp.full(m_sc.shape, NEG, F32)
        l_sc[...] = jnp.zeros(l_sc.shape, F32)
        acc_sc[...] = jnp.zeros(acc_sc.shape, F32)

    def step(masked):
        kb = k_ref[...].astype(BF16)
        vb = v_ref[...].astype(BF16)
        s = _dot_nt(qx_sc[...], kb)
        if masked:
            row = lax.broadcasted_iota(I32, s.shape, 0)
            col = lax.broadcasted_iota(I32, s.shape, 1)
            s = jnp.where((row % tq) >= col, s, NEG)
        m_old = m_sc[...]
        m_new = jnp.maximum(m_old, jnp.max(s, axis=-1, keepdims=True))
        p = jnp.exp2(s - jnp.concatenate([m_new] * (tq // LANES), axis=1))
        alpha = jnp.exp2(m_old - m_new)
        l_sc[...] = alpha * l_sc[...] + jnp.sum(p, axis=-1, keepdims=True)
        acc_sc[...] = alpha * acc_sc[...] + _dot(p.astype(BF16), vb)
        m_sc[...] = m_new

    @pl.when(ki < qi)
    def _below_diagonal():
        step(False)

    @pl.when(ki == qi)
    def _diagonal():
        step(True)
        lam = lam_ref[0]
        of = acc_sc[...] / l_sc[...]
        o = of[0:tq] - lam * of[tq:2 * tq]
        o_ref[...] = (_rms(o, g_ref[...]) * out_scale).astype(o_ref.dtype)


def _flash_diff(lam, q, k, v, subln_g, nb, out_scale, out_dtype):
    t = q.shape[0] // nb
    tq = min(ATT_TQ, t)
    nq = t // tq
    pairs = [(i, j) for i in range(nq) for j in range(i + 1)]
    qi_tab = jnp.asarray([p[0] for p in pairs], I32)
    ki_tab = jnp.asarray([p[1] for p in pairs], I32)
    qmap = lambda b, h, p, qt, kt, lm: (b * nq + qt[p], h)
    kmap = lambda b, h, p, qt, kt, lm: (b * nq + kt[p], h)
    blk = 2 * DH_C
    grid_spec = pltpu.PrefetchScalarGridSpec(
        num_scalar_prefetch=3,
        grid=(nb, H_C, len(pairs)),
        in_specs=[pl.BlockSpec((tq, blk), qmap), pl.BlockSpec((tq, blk), kmap), pl.BlockSpec((tq, blk), kmap),
                  pl.BlockSpec((1, blk), lambda b, h, p, qt, kt, lm: (0, 0))],
        out_specs=pl.BlockSpec((tq, blk), qmap),
        scratch_shapes=[pltpu.VMEM((2 * tq, blk), BF16), pltpu.VMEM((2 * tq, LANES), F32),
                        pltpu.VMEM((2 * tq, LANES), F32), pltpu.VMEM((2 * tq, blk), F32)],
    )
    return pl.pallas_call(
        functools.partial(_flash_diff_kernel, tq=tq, out_scale=out_scale),
        grid_spec=grid_spec,
        out_shape=jax.ShapeDtypeStruct((q.shape[0], D_C), out_dtype),
        compiler_params=_cparams("parallel", "parallel", "arbitrary"),
        name="flash_diff",
    )(qi_tab, ki_tab, lam, q, k, v, subln_g)


def _paged_diff_kernel(pt_ref, lam_ref, q_ref, kt_ref, vc_ref, kn_ref, vn_ref, g_ref, o_ref,
                       qx_sc, m_sc, l_sc, acc_sc, *, n_pages, tdec, out_scale):
    p = pl.program_id(1)
    nrow = 2 * H_C * tdec

    @pl.when(p == 0)
    def _init():
        q = q_ref[...].astype(F32)
        qt = jnp.concatenate([q] * (2 * H_C), axis=0)
        r = lax.broadcasted_iota(I32, (nrow, D_C), 0)
        c = lax.broadcasted_iota(I32, (nrow, D_C), 1)
        qx_sc[...] = jnp.where((r // tdec) == (c // DH_C), qt, 0.0).astype(BF16)
        m_sc[...] = jnp.full(m_sc.shape, NEG, F32)
        l_sc[...] = jnp.zeros(l_sc.shape, F32)
        acc_sc[...] = jnp.zeros(acc_sc.shape, F32)

    hrows = 2 * tdec

    def update(s, head_v):
        m_old = m_sc[...]
        m_new = jnp.maximum(m_old, jnp.max(s, axis=-1, keepdims=True))
        pr = jnp.exp2(s - m_new[:, 0:s.shape[1]])
        alpha = jnp.exp2(m_old - m_new)
        l_sc[...] = alpha * l_sc[...] + jnp.sum(pr, axis=-1, keepdims=True)
        for h in range(H_C):
            rs = slice(h * hrows, (h + 1) * hrows)
            vh = head_v(h)
            acc_sc[rs, :] = alpha[rs] * acc_sc[rs, :] + _dot(pr[rs].astype(vh.dtype), vh)
        m_sc[...] = m_new

    @pl.when(p < n_pages)
    def _page():
        s = _dot(qx_sc[...], kt_ref[...].astype(BF16))
        update(s, lambda h: vc_ref[pl.ds(h, PAGE_SIZE, stride=H_C), :].astype(BF16))

    @pl.when(p == n_pages)
    def _new_rows():
        s = _dot_nt(qx_sc[...].astype(F32), kn_ref[...])
        r = lax.broadcasted_iota(I32, s.shape, 0)
        c = lax.broadcasted_iota(I32, s.shape, 1)
        s = jnp.where(c <= (r % tdec), s, NEG)
        update(s, lambda h: vn_ref[:, h * 2 * DH_C:(h + 1) * 2 * DH_C])
        lam = lam_ref[0]
        of = acc_sc[...] / l_sc[...]
        for h in range(H_C):
            o1 = of[h * hrows:h * hrows + tdec]
            o2 = of[h * hrows + tdec:(h + 1) * hrows]
            o = o1 - lam * o2
            o_ref[:, h * 2 * DH_C:(h + 1) * 2 * DH_C] = (_rms(o, g_ref[...]) * out_scale).astype(o_ref.dtype)


def _paged_diff(page_table, lam, q, cache_kt, cache_v2, k_new, v_new, subln_g, ic, tdec, out_scale):
    nb, n_pages = page_table.shape
    nrow = 2 * H_C * tdec
    rows = lambda b, p, pt, lm: (b, 0)
    page = lambda b, p, pt, lm: (ic, pt[b * n_pages + jnp.minimum(p, n_pages - 1)], 0, 0)
    grid_spec = pltpu.PrefetchScalarGridSpec(
        num_scalar_prefetch=2,
        grid=(nb, n_pages + 1),
        in_specs=[pl.BlockSpec((tdec, D_C), rows),
                  pl.BlockSpec((None, None, D_C, PAGE_SIZE), page),
                  pl.BlockSpec((None, None, PAGE_SIZE * H_C, 2 * DH_C), page),
                  pl.BlockSpec((tdec, D_C), rows), pl.BlockSpec((tdec, D_C), rows),
                  pl.BlockSpec((1, 2 * DH_C), lambda b, p, pt, lm: (0, 0))],
        out_specs=pl.BlockSpec((tdec, D_C), rows),
        scratch_shapes=[pltpu.VMEM((nrow, D_C), BF16), pltpu.VMEM((nrow, LANES), F32),
                        pltpu.VMEM((nrow, LANES), F32), pltpu.VMEM((nrow, 2 * DH_C), F32)],
    )
    return pl.pallas_call(
        functools.partial(_paged_diff_kernel, n_pages=n_pages, tdec=tdec, out_scale=out_scale),
        grid_spec=grid_spec,
        out_shape=jax.ShapeDtypeStruct((nb * tdec, D_C), F32),
        compiler_params=_cparams("parallel", "arbitrary"),
        name="paged_diff",
    )(page_table.reshape(-1), lam, q, cache_kt, cache_v2, k_new, v_new, subln_g)


def _router_kernel(x_ref, g_ref, wr_ref, br_ref, base_ref, xn_ref, ri_ref, rw_ref, cnt_ref, cnt_sc):
    i = pl.program_id(0)
    tm = x_ref.shape[0]

    @pl.when(i == 0)
    def _init():
        cnt_sc[...] = base_ref[...]

    xn = _rms(x_ref[...], g_ref[...])
    xn_ref[...] = xn
    lg = lax.dot_general(wr_ref[...], xn, (((1,), (1,)), ((), ())), preferred_element_type=F32,
                         precision=lax.Precision.HIGHEST) + br_ref[:, 0:1]
    sub = lax.broadcasted_iota(I32, (SUBLANES, tm), 0)
    gl = lg[N_EXPERTS:N_EXPERTS + SUBLANES, :]
    gmax = jnp.max(gl, axis=0, keepdims=True)
    gsel = jnp.min(jnp.where(gl == gmax, sub, N_EXPERTS), axis=0, keepdims=True)
    g_w = 1.0 / jnp.sum(jnp.exp(gl - gmax), axis=0, keepdims=True)
    el = lg[0:E_PER_GROUP, :]
    for gi in range(1, N_GROUPS):
        el = jnp.where(gsel == gi, lg[gi * E_PER_GROUP:(gi + 1) * E_PER_GROUP, :], el)
    l1 = jnp.max(el, axis=0, keepdims=True)
    i1 = jnp.min(jnp.where(el == l1, sub, N_EXPERTS), axis=0, keepdims=True)
    el2 = jnp.where(sub == i1, NEG, el)
    l2 = jnp.max(el2, axis=0, keepdims=True)
    i2 = jnp.min(jnp.where(el2 == l2, sub, N_EXPERTS), axis=0, keepdims=True)
    p2 = jnp.exp(l2 - l1)
    w1 = g_w / (1.0 + p2)
    w2 = g_w * p2 / (1.0 + p2)
    e1 = gsel * E_PER_GROUP + i1
    e2 = gsel * E_PER_GROUP + i2

    erow = lax.broadcasted_iota(I32, (N_EXPERTS, tm), 0)
    oh = ((erow == e1) | (erow == e2))
    tr = lax.broadcasted_iota(I32, (tm, tm), 0)
    tc = lax.broadcasted_iota(I32, (tm, tm), 1)
    before = (tr < tc).astype(BF16)
    cum = _dot(oh.astype(BF16), before) + cnt_sc[:, 0:1]
    r1 = jnp.sum(jnp.where(erow == e1, cum, 0.0), axis=0, keepdims=True)
    r2 = jnp.sum(jnp.where(erow == e2, cum, 0.0), axis=0, keepdims=True)
    cnt_sc[...] = cnt_sc[...] + jnp.sum(oh.astype(F32), axis=1, keepdims=True)

    ri_ref[...] = jnp.where(sub == 0, e1, jnp.where(sub == 1, e2, jnp.where(
        sub == 2, r1.astype(I32), jnp.where(sub == 3, r2.astype(I32), 0))))
    rw_ref[...] = jnp.where(sub == 0, w1, jnp.where(sub == 1, w2, 0.0))
    cnt_ref[...] = cnt_sc[...]


def _router(x, g, w_rt, b_rt, base_cnt, tm):
    n = x.shape[0]
    return pl.pallas_call(
        _router_kernel,
        grid=(n // tm,),
        in_specs=[pl.BlockSpec((tm, D_MODEL), lambda i: (i, 0)), _full((1, D_MODEL)),
                  _full((LANES, D_MODEL)), _full((LANES, LANES)), _full((N_EXPERTS, LANES))],
        out_specs=[pl.BlockSpec((tm, D_MODEL), lambda i: (i, 0)),
                   pl.BlockSpec((SUBLANES, tm), lambda i: (0, i)),
                   pl.BlockSpec((SUBLANES, tm), lambda i: (0, i)),
                   _full((N_EXPERTS, LANES))],
        out_shape=[jax.ShapeDtypeStruct((n, D_MODEL), F32),
                   jax.ShapeDtypeStruct((SUBLANES, n), I32),
                   jax.ShapeDtypeStruct((SUBLANES, n), F32),
                   jax.ShapeDtypeStruct((N_EXPERTS, LANES), F32)],
        scratch_shapes=[pltpu.VMEM((N_EXPERTS, LANES), F32)],
        compiler_params=_cparams("arbitrary"),
        name="moe_router",
    )(x, g, w_rt, b_rt, base_cnt)


def _dispatch_kernel(pos_ref, xn_ref, xs_in_ref, xs_ref, sem):
    del xs_in_ref
    tm = xn_ref.shape[0]

    def row_copy(t, j):
        return pltpu.make_async_copy(xn_ref.at[pl.ds(t, 1), :],
                                     xs_ref.at[pl.ds(pos_ref[0, 0, j * tm + t], 1), :], sem)

    def issue(t, c):
        row_copy(t, 0).start()
        row_copy(t, 1).start()
        return c

    def drain(t, c):
        row_copy(t, 0).wait()
        row_copy(t, 1).wait()
        return c

    lax.fori_loop(0, tm, issue, 0)
    lax.fori_loop(0, tm, drain, 0)


def _dispatch(pos3, xn, xs, tm):
    n = xn.shape[0]
    return pl.pallas_call(
        _dispatch_kernel,
        grid=(n // tm,),
        in_specs=[pl.BlockSpec((1, 1, 2 * tm), lambda i: (i, 0, 0), memory_space=pltpu.SMEM),
                  pl.BlockSpec((tm, D_MODEL), lambda i: (i, 0)),
                  pl.BlockSpec(memory_space=pl.ANY)],
        out_specs=pl.BlockSpec(memory_space=pl.ANY),
        out_shape=jax.ShapeDtypeStruct(xs.shape, xs.dtype),
        scratch_shapes=[pltpu.SemaphoreType.DMA(())],
        input_output_aliases={2: 0},
        compiler_params=_cparams("arbitrary"),
        name="moe_dispatch",
    )(pos3, xn, xs)


def _experts_kernel(te_ref, nt_ref, xs_ref, wg_ref, wu_ref, wd_ref, o_ref):
    i = pl.program_id(0)

    @pl.when(i < nt_ref[0])
    def _run():
        x = xs_ref[...].astype(BF16)
        gate = _dot(x, wg_ref[...].astype(BF16))
        up = _dot(x, wu_ref[...].astype(BF16))
        h = gate * jax.nn.sigmoid(gate) * up
        o_ref[...] = _dot(h.astype(BF16), wd_ref[...].astype(BF16))

    @pl.when(i >= nt_ref[0])
    def _unused_tile():
        o_ref[...] = jnp.zeros(o_ref.shape, o_ref.dtype)


def _experts(tile_expert, n_tiles, xs, w_gate, w_up, w_down, layer, te):
    max_tiles = xs.shape[0] // te
    clamp = lambda i, nt: jnp.minimum(i, nt[0] - 1)
    rows = lambda i, te_ref, nt: (clamp(i, nt), 0)

    def wsel(i, te_ref, nt):
        e = te_ref[clamp(i, nt)]
        return (layer, e // E_PER_GROUP, e % E_PER_GROUP, 0, 0)

    grid_spec = pltpu.PrefetchScalarGridSpec(
        num_scalar_prefetch=2,
        grid=(max_tiles,),
        in_specs=[pl.BlockSpec((te, D_MODEL), rows),
                  pl.BlockSpec((None, None, None, D_MODEL, D_FF), wsel),
                  pl.BlockSpec((None, None, None, D_MODEL, D_FF), wsel),
                  pl.BlockSpec((None, None, None, D_FF, D_MODEL), wsel)],
        out_specs=pl.BlockSpec((te, D_MODEL), lambda i, te_ref, nt: (i, 0)),
    )
    return pl.pallas_call(
        _experts_kernel,
        grid_spec=grid_spec,
        out_shape=jax.ShapeDtypeStruct(xs.shape, F32),
        compiler_params=_cparams("arbitrary"),
        name="moe_experts",
    )(tile_expert, n_tiles, xs, w_gate, w_up, w_down)


def _combine_kernel(pos_ref, x_ref, rw_ref, g_ref, ys_ref, o_ref, buf, sem, *, final_norm):
    tm = x_ref.shape[0]

    def row_copy(t, j):
        return pltpu.make_async_copy(ys_ref.at[pl.ds(pos_ref[0, 0, j * tm + t], 1), :],
                                     buf.at[j, pl.ds(t, 1), :], sem)

    def issue(t, c):
        row_copy(t, 0).start()
        row_copy(t, 1).start()
        return c

    def drain(t, c):
        row_copy(t, 0).wait()
        row_copy(t, 1).wait()
        return c

    lax.fori_loop(0, tm, issue, 0)
    wt = jnp.concatenate([rw_ref[...], jnp.zeros((LANES - SUBLANES, tm), F32)], axis=0).T
    lax.fori_loop(0, tm, drain, 0)
    y = x_ref[...] + wt[:, 0:1] * buf[0] + wt[:, 1:2] * buf[1]
    if final_norm:
        y = _rms(y, g_ref[...])
    o_ref[...] = y


def _combine(pos3, x, rw, g, ys, tm, final_norm):
    n = x.shape[0]
    return pl.pallas_call(
        functools.partial(_combine_kernel, final_norm=final_norm),
        grid=(n // tm,),
        in_specs=[pl.BlockSpec((1, 1, 2 * tm), lambda i: (i, 0, 0), memory_space=pltpu.SMEM),
                  pl.BlockSpec((tm, D_MODEL), lambda i: (i, 0)),
                  pl.BlockSpec((SUBLANES, tm), lambda i: (0, i)),
                  _full((1, D_MODEL)),
                  pl.BlockSpec(memory_space=pl.ANY)],
        out_specs=pl.BlockSpec((tm, D_MODEL), lambda i: (i, 0)),
        out_shape=jax.ShapeDtypeStruct((n, D_MODEL), F32),
        scratch_shapes=[pltpu.VMEM((2, tm, D_MODEL), F32), pltpu.SemaphoreType.DMA(())],
        compiler_params=_cparams("arbitrary"),
        name="moe_combine",
    )(pos3, x, rw, g, ys)


def _moe(xs_groups, g, w_rt, b_rt, w_gate, w_up, w_down, layer, xs_buf, final_g):
    tm, te = MOE_TM, MOE_TE
    base = jnp.zeros((N_EXPERTS, LANES), F32)
    routed = []
    for x in xs_groups:
        xn, ri, rw, base = _router(x, g, w_rt, b_rt, base, tm)
        routed.append((xn, ri, rw))
    counts = base[:, 0].astype(I32)
    padded = ((counts + te - 1) // te) * te
    ends = jnp.cumsum(padded)
    starts = ends - padded
    max_tiles = xs_buf.shape[0] // te
    tile_ids = jnp.arange(max_tiles, dtype=I32)
    tile_expert = jnp.minimum(jnp.sum((ends[None, :] // te <= tile_ids[:, None]).astype(I32), axis=1),
                              N_EXPERTS - 1).astype(I32)
    n_tiles = (ends[-1:] // te).astype(I32)
    expert_ids = jnp.arange(N_EXPERTS, dtype=I32)
    pos3s = []
    for xn, ri, rw in routed:
        n = xn.shape[0]
        start_of = jnp.sum(jnp.where(ri[0:2, :, None] == expert_ids, starts, 0), axis=-1)
        pos = start_of + ri[2:4]
        pos3 = pos.reshape(2, n // tm, tm).transpose(1, 0, 2).reshape(n // tm, 1, 2 * tm)
        pos3s.append(pos3)
        xs_buf = _dispatch(pos3, xn, xs_buf, tm)
    ys = _experts(tile_expert, n_tiles, xs_buf, w_gate, w_up, w_down, layer, te)
    outs = []
    for x, pos3, (xn, ri, rw) in zip(xs_groups, pos3s, routed):
        gg = final_g if final_g is not None else g
        outs.append(_combine(pos3, x, rw, gg, ys, tm, final_g is not None))
    return outs, xs_buf


def _rope_tables(pos):
    half = DH_C // 2
    inv_freq = jnp.exp(-math.log(ROPE_THETA) * jnp.arange(half, dtype=F32) / half)
    ang = pos.astype(F32)[:, None] * inv_freq[None, :]
    cos = jnp.cos(ang)
    sin = jnp.sin(ang)
    cos_t = jnp.tile(jnp.concatenate([cos, cos], axis=1), (1, LANES // DH_C))
    sin_t = jnp.tile(jnp.concatenate([-sin, sin], axis=1), (1, LANES // DH_C))
    return cos_t, sin_t


def kernel(x_prompt, x_sample, state_mlstm_C, state_mlstm_n, state_mlstm_m, state_conv, cache_k, cache_v, page_table, cache_mem_k, cache_mem_v, mem_prompt, norm_mix_g, norm_mem_g, norm_ffn_g, final_norm_g, w_in_a, gate_b_a, head_norm_g_a, conv_w_b, w_out_a, w_qkv_c, lambda_c, subln_g_c, w_out_c, w_mem_q, w_mem_k, w_mem_v, w_mem_o, w_router_g, b_router_g, w_router_e, b_router_e, w_exp_gate, w_exp_up, w_exp_down):
    bp, tp, d = x_prompt.shape
    bs, ts, _ = x_sample.shape
    depth = norm_mix_g.shape[0]
    n_p, n_s = bp * tp, bs * ts
    past_len = page_table.shape[1] * PAGE_SIZE
    assert d == D_MODEL and ts == SUBLANES
    tm_p = min(ROW_TILE, n_p)
    tm_s = min(ROW_TILE, n_s)

    xp = x_prompt.reshape(n_p, d)
    xs = x_sample.reshape(n_s, d)
    row = lambda v: v.reshape(1, -1).astype(F32)

    mem_rows = mem_prompt.reshape(bp * N_MEM, d)
    tm_m = min(ROW_TILE, bp * N_MEM)
    p_mem_k = [_mm(mem_rows, w_mem_k[l].astype(BF16), F32, tm_m) for l in range(depth)]
    p_mem_v = [_mm(mem_rows, w_mem_v[l].astype(BF16), F32, tm_m) for l in range(depth)]
    LOG2E = math.log2(math.e)

    n_all = n_p + n_s
    xs_buf = jnp.zeros((2 * n_all + N_EXPERTS * MOE_TE, d), F32)

    out_c_p, out_n_p, out_m_p, out_conv_p, out_k_p, out_v_p = [], [], [], [], [], []
    out_c_s, out_n_s, out_m_s, out_conv_s, out_k_s, out_v_s = [], [], [], [], [], []
    ia = ic = 0
    for layer in range(depth):
        g_mix = row(norm_mix_g[layer])
        if layer % 2 == 0:
            w = w_in_a[ia]
            k_scale = DH_A ** -0.5
            w_main = jnp.concatenate([w[:, 0:D_A], w[:, D_A:2 * D_A] * k_scale, w[:, 2 * D_A:4 * D_A],
                                      w[:, 4 * D_A + 2 * H_A:]], axis=1).astype(BF16)
            w_gate = jnp.pad(w[:, 4 * D_A:4 * D_A + 2 * H_A], ((0, 0), (0, LANES - 2 * H_A))).astype(BF16)
            gbias = jnp.pad(gate_b_a[ia].reshape(1, 2 * H_A).astype(F32), ((0, 0), (0, LANES - 2 * H_A)))
            head_g = row(head_norm_g_a[ia])
            conv_w = jnp.pad(conv_w_b[ia].astype(F32), ((0, SUBLANES - CONV_W), (0, 0)))
            w_out = w_out_a[ia].astype(BF16)

            qkvo, cv, gt = _inproj(xp, g_mix, w_main, w_gate, BF16, tm_p)
            zeros = lambda *s: jnp.zeros(s, F32)
            y, c_n, n_n, m_n, cb_n = _mlstm(
                qkvo, cv, gt, gbias, head_g, conv_w,
                zeros(bp, H_A, DH_A, DH_A), zeros(bp, SUBLANES, DH_A), zeros(bp, SUBLANES, LANES),
                zeros(bp, SUBLANES, D_B), min(MLSTM_L, tp), BF16)
            xp = _mm_res(xp, y, w_out, tm_p)
            out_c_p.append(c_n)
            out_n_p.append(n_n[:, 0:H_A])
            out_m_p.append(m_n[:, 0:H_A, 0])
            out_conv_p.append(cb_n[:, SUBLANES - (CONV_W - 1):])

            qkvo, cv, gt = _inproj(xs, g_mix, w_main, w_gate, F32, tm_s)
            n0 = jnp.pad(state_mlstm_n[ia].astype(F32), ((0, 0), (0, SUBLANES - H_A), (0, 0)))
            m0 = jnp.pad(jnp.broadcast_to(state_mlstm_m[ia].astype(F32)[:, :, None], (bs, H_A, LANES)),
                         ((0, 0), (0, SUBLANES - H_A), (0, 0)))
            cb0 = jnp.pad(state_conv[ia].astype(F32), ((0, 0), (SUBLANES - (CONV_W - 1), 0), (0, 0)))
            y, c_n, n_n, m_n, cb_n = _mlstm(
                qkvo, cv, gt, gbias, head_g, conv_w,
                state_mlstm_C[ia].astype(F32), n0, m0, cb0, ts, F32)
            xs = _mm_res(xs, y, w_out, tm_s)
            out_c_s.append(c_n)
            out_n_s.append(n_n[:, 0:H_A])
            out_m_s.append(m_n[:, 0:H_A, 0])
            out_conv_s.append(cb_n[:, SUBLANES - (CONV_W - 1):])
            ia += 1
        else:
            w_qkv = w_qkv_c[ic].astype(BF16)
            w_out = w_out_c[ic].astype(BF16)
            subln_g = row(subln_g_c[ic])
            lam_init = 0.8 - 0.6 * math.exp(-0.3 * layer)
            lv = lambda_c[ic].astype(F32)
            lam = (jnp.exp(jnp.sum(lv[0] * lv[1])) - jnp.exp(jnp.sum(lv[2] * lv[3])) + lam_init).reshape(1)
            out_scale = 1.0 - lam_init

            cos_p, sin_p = _rope_tables(jnp.arange(tp))
            npt = tp // min(tm_p, tp)
            tmq = min(tm_p, tp)
            q, k, v = _qkv_rope(xp, g_mix, w_qkv, cos_p, sin_p, lambda i: (i % npt, 0), BF16, tmq)
            o = _flash_diff(lam, q, k, v, subln_g, bp, out_scale, BF16)
            xp = _mm_res(xp, o, w_out, tm_p)
            out_k_p.append(k)
            out_v_p.append(v)

            cos_s, sin_s = _rope_tables(past_len + jnp.arange(ts))
            cos_s = jnp.tile(cos_s, (tm_s // ts, 1))
            sin_s = jnp.tile(sin_s, (tm_s // ts, 1))
            q, k, v = _qkv_rope(xs, g_mix, w_qkv, cos_s, sin_s, lambda i: (0, 0), F32, tm_s)
            n_pool = cache_k.shape[1]
            cache_kt = jnp.transpose(cache_k, (0, 1, 3, 4, 5, 2)).reshape(-1, n_pool, D_C, PAGE_SIZE)
            cache_v2 = cache_v.reshape(-1, n_pool, PAGE_SIZE * H_C, 2 * DH_C)
            o = _paged_diff(page_table, lam, q, cache_kt, cache_v2, k, v, subln_g, ic, ts, out_scale)
            xs = _mm_res(xs, o, w_out, tm_s)
            out_k_s.append(k)
            out_v_s.append(v)
            ic += 1

        g_mem = row(norm_mem_g[layer])
        w_q = w_mem_q[layer].astype(BF16)
        w_o = w_mem_o[layer].astype(BF16)
        q_scale = DH_M ** -0.5 * LOG2E
        q = _norm_mm(xp, g_mem, w_q, BF16, tm_p, scale=q_scale)
        p_heads = [pl.BlockSpec((N_MEM, DH_M), functools.partial(lambda b, i, h: (b, h), h=h)) for h in range(H_M)]
        o = _mem_attn(q, p_mem_k[layer], p_mem_v[layer], p_heads, bp, min(MEM_TQ, tp), BF16)
        xp = _mm_res(xp, o, w_o, tm_p)
        q = _norm_mm(xs, g_mem, w_q, F32, tm_s, scale=q_scale)
        s_block = [pl.BlockSpec((None, None, N_MEM, H_M, DH_M),
                                functools.partial(lambda b, i, l: (l, b, 0, 0, 0), l=layer))]
        o = _mem_attn(q, cache_mem_k, cache_mem_v, s_block, bs, ts, F32)
        xs = _mm_res(xs, o, w_o, tm_s)

        g_ffn = row(norm_ffn_g[layer])
        w_rt = jnp.zeros((LANES, d), F32)
        w_rt = w_rt.at[0:N_EXPERTS].set(w_router_e[layer].astype(F32).T)
        w_rt = w_rt.at[N_EXPERTS:N_EXPERTS + N_GROUPS].set(w_router_g[layer].astype(F32).T)
        b_col = jnp.zeros((LANES,), F32).at[0:N_EXPERTS].set(b_router_e[layer].astype(F32))
        b_col = b_col.at[N_EXPERTS:N_EXPERTS + N_GROUPS].set(b_router_g[layer].astype(F32))
        b_col = b_col.at[N_EXPERTS + N_GROUPS:N_EXPERTS + SUBLANES].set(NEG)
        b_rt = jnp.broadcast_to(b_col[:, None], (LANES, LANES))
        final_g = row(final_norm_g) if layer == depth - 1 else None
        (xp, xs), xs_buf = _moe([xp, xs], g_ffn, w_rt, b_rt, w_exp_gate, w_exp_up, w_exp_down, layer,
                                xs_buf, final_g)

    y_prompt = xp.reshape(bp, tp, d)
    y_sample = xs.reshape(bs, ts, d)
    stk = jnp.stack
    return (y_prompt, y_sample,
            stk(out_c_p), stk(out_n_p), stk(out_m_p), stk(out_conv_p),
            stk(out_k_p).reshape(-1, bp, tp, H_C, 2, DH_C), stk(out_v_p).reshape(-1, bp, tp, H_C, 2 * DH_C),
            stk(p_mem_k).reshape(depth, bp, N_MEM, H_M, DH_M), stk(p_mem_v).reshape(depth, bp, N_MEM, H_M, DH_M),
            stk(out_c_s), stk(out_n_s), stk(out_m_s), stk(out_conv_s),
            stk(out_k_s).reshape(-1, bs, ts, H_C, 2, DH_C), stk(out_v_s).reshape(-1, bs, ts, H_C, 2 * DH_C))
```
